```python
import functools
import jax, jax.numpy as jnp
from jax import lax
import numpy as np

D_MODEL = 1024
BATCH = 16
SEQ = 256
DEPTH = 4
DEC_BATCH = 8
DEC_SEQ = 2048
PAST_LEN = 256

GRID_W = 64
NA_HEADS = 16
HEAD_DIM = 64
D_ATTN = NA_HEADS * HEAD_DIM
NA_KH = 8
NA_KW = 16
RPB_H = 2 * NA_KH - 1
RPB_W = 2 * NA_KW - 1
D_CONV = D_MODEL
CONV_K = 31
D_SC = D_MODEL
SC_K = 3
N_BRANCH = 3
D_FF = ((8 * D_MODEL + 3 * 256 - 1) // (3 * 256)) * 256
D_IN = 2 * D_CONV + 3 * D_ATTN + 3 * D_SC + N_BRANCH * D_MODEL
Q_BLOCK = 128
EPS = 1e-6
ATTN_SCALE = HEAD_DIM ** -0.5

kernel_name = 'hybrid_dit_conformer_natten_shortconv'


def rmsnorm(x, g):
    xf = x.astype(jnp.float32)
    xf = xf * lax.rsqrt(jnp.mean(xf * xf, axis=-1, keepdims=True) + EPS)
    return xf.astype(x.dtype) * g


def layernorm(x, g, b):
    xf = x.astype(jnp.float32)
    mu = jnp.mean(xf, axis=-1, keepdims=True)
    var = jnp.mean(jnp.square(xf - mu), axis=-1, keepdims=True)
    return ((xf - mu) * lax.rsqrt(var + EPS)).astype(x.dtype) * g + b


def depthwise_conv(x, w):
    k = w.shape[0]
    pad = (k - 1) // 2
    return lax.conv_general_dilated(x, w[:, None, :], window_strides=(1,), padding=[(pad, pad)],
                                    dimension_numbers=('NWC', 'WIO', 'NWC'),
                                    feature_group_count=x.shape[-1])


def adaln(cond, w_ada, b_ada):
    m = jax.nn.silu(cond) @ w_ada + b_ada
    return jnp.split(m, 6, axis=-1)


def conformer_conv(glu_in, w_dw, b_dw, ln_g, ln_b, w_pw):
    a, g = jnp.split(glu_in, 2, axis=-1)
    h = depthwise_conv(a * jax.nn.sigmoid(g), w_dw) + b_dw
    h = jax.nn.silu(layernorm(h, ln_g, ln_b))
    return h @ w_pw


def short_conv(sc_in, w_dw, w_out):
    bg, cg, xs = jnp.split(sc_in, 3, axis=-1)
    return (bg * depthwise_conv(cg * xs, w_dw)) @ w_out


def context_attention(q, k, v):
    b, s, h, dh = q.shape
    qb = jnp.moveaxis(q.reshape(b, s // Q_BLOCK, Q_BLOCK, h, dh), 1, 0)

    def block(qi):
        sc = jnp.einsum('bqhd,bhkd->bhqk', qi, k).astype(jnp.float32) * ATTN_SCALE
        p = jax.nn.softmax(sc, axis=-1).astype(v.dtype)
        return jnp.einsum('bhqk,bhkd->bqhd', p, v)

    out = lax.map(block, qb)
    return jnp.moveaxis(out, 0, 1).reshape(b, s, h * dh)


def neighbourhood_attention(q, k, v, k_ctx, v_ctx, rpb):
    b, t, h, dh = q.shape
    rows = t // GRID_W
    kh = min(NA_KH, rows)
    n_loc = kh * NA_KW
    qg = q.reshape(b, rows, GRID_W, h, dh)
    kg = k.reshape(b, rows, GRID_W, h, dh)
    vg = v.reshape(b, rows, GRID_W, h, dh)
    cols = np.arange(GRID_W)
    col_start = np.clip(cols - NA_KW // 2, 0, GRID_W - NA_KW)
    col_idx = col_start[:, None] + np.arange(NA_KW)[None, :]
    dcol = col_idx - cols[:, None] + (NA_KW - 1)
    rpb_cols = rpb[:, :, dcol]

    def row_block(r):
        rs = jnp.clip(r - kh // 2, 0, rows - kh)
        q_r = lax.dynamic_index_in_dim(qg, r, axis=1, keepdims=False)
        k_win = lax.dynamic_slice_in_dim(kg, rs, kh, axis=1)[:, :, col_idx]
        v_win = lax.dynamic_slice_in_dim(vg, rs, kh, axis=1)[:, :, col_idx]
        drow = rs + jnp.arange(kh) - r + (NA_KH - 1)
        bias = jnp.transpose(rpb_cols[:, drow], (0, 2, 1, 3))
        s_loc = (jnp.einsum('bwhd,bkwjhd->bhwkj', q_r, k_win).astype(jnp.float32) * ATTN_SCALE
                 + bias.astype(jnp.float32))
        s_ctx = jnp.einsum('bwhd,bhld->bhwl', q_r, k_ctx).astype(jnp.float32) * ATTN_SCALE
        p = jax.nn.softmax(jnp.concatenate([s_loc.reshape(b, h, GRID_W, n_loc), s_ctx], axis=-1),
                           axis=-1).astype(v.dtype)
        p_loc = p[..., :n_loc].reshape(b, h, GRID_W, kh, NA_KW)
        return (jnp.einsum('bhwkj,bkwjhd->bwhd', p_loc, v_win)
                + jnp.einsum('bhwl,bhld->bwhd', p[..., n_loc:], v_ctx))

    out = lax.map(row_block, jnp.arange(rows))
    return jnp.moveaxis(out, 0, 1).reshape(b, t, h * dh)


def attend_context(q, k, v, rpb):
    kc = jnp.transpose(k, (0, 2, 1, 3))
    vc = jnp.transpose(v, (0, 2, 1, 3))
    return context_attention(q, kc, vc), kc, vc


def attend_latent(q, k, v, rpb, k_ctx, v_ctx):
    return neighbourhood_attention(q, k, v, k_ctx, v_ctx, rpb), None, None


def trunk_layer(x, cond, p, attend):
    shift1, scale1, gate1, shift2, scale2, gate2 = adaln(cond, p['w_ada'], p['b_ada'])
    u = rmsnorm(x, p['norm1_g']) * (1 + scale1) + shift1
    proj = u @ p['w_in']
    o1 = 2 * D_CONV
    o2 = o1 + 3 * D_ATTN
    o3 = o2 + 3 * D_SC
    glu_in, qkv, sc_in, gate_in = proj[..., :o1], proj[..., o1:o2], proj[..., o2:o3], proj[..., o3:]
    y_conv = conformer_conv(glu_in, p['conv_dw_w'], p['conv_dw_b'], p['conv_ln_g'], p['conv_ln_b'], p['conv_pw_w'])
    y_sc = short_conv(sc_in, p['sc_dw_w'], p['sc_out_w'])
    b, t = x.shape[0], x.shape[1]
    q, k, v = [z.reshape(b, t, NA_HEADS, HEAD_DIM) for z in jnp.split(qkv, 3, axis=-1)]
    y_attn, k_keep, v_keep = attend(q, k, v, p['na_rpb'])
    y_na = y_attn @ p['na_out_w']
    g_conv, g_na, g_sc = jnp.split(jax.nn.sigmoid(gate_in), 3, axis=-1)
    mixed = (g_conv * y_conv + g_na * y_na + g_sc * y_sc) @ p['w_o']
    x = x + gate1 * mixed
    u2 = rmsnorm(x, p['norm2_g']) * (1 + scale2) + shift2
    ffn = (jax.nn.silu(u2 @ p['ffn_w_gate']) * (u2 @ p['ffn_w_up'])) @ p['ffn_w_down']
    x = x + gate2 * ffn
    return x, k_keep, v_keep


def setup_inputs(seed: int = 0) -> dict:
    key = jax.random.key(seed)
    ks = jax.random.split(key, 26)

    def nrm(k, shape, scale):
        return jax.random.normal(k, shape, jnp.float32) * scale

    f = D_MODEL ** -0.5
    return {
        'x_prompt': nrm(ks[0], (BATCH, SEQ, D_MODEL), 1.0),
        'x_sample': nrm(ks[1], (DEC_BATCH, DEC_SEQ, D_MODEL), 1.0),
        'cache_k': nrm(ks[2], (DEC_BATCH, DEPTH, NA_HEADS, PAST_LEN, HEAD_DIM), 1.0),
        'cache_v': nrm(ks[3], (DEC_BATCH, DEPTH, NA_HEADS, PAST_LEN, HEAD_DIM), 1.0),
        'c': nrm(ks[4], (DEC_BATCH, D_MODEL), 1.0),
        'c_ctx': nrm(ks[5], (D_MODEL,), 1.0),
        'w_ada': nrm(ks[6], (DEPTH, D_MODEL, 6 * D_MODEL), 0.5 * f),
        'b_ada': nrm(ks[7], (DEPTH, 6 * D_MODEL), 0.02),
        'norm1_g': 1.0 + nrm(ks[8], (DEPTH, D_MODEL), 0.02),
        'w_in': nrm(ks[9], (DEPTH, D_MODEL, D_IN), f),
        'conv_dw_w': nrm(ks[10], (DEPTH, CONV_K, D_CONV), CONV_K ** -0.5),
        'conv_dw_b': nrm(ks[11], (DEPTH, D_CONV), 0.02),
        'conv_ln_g': 1.0 + nrm(ks[12], (DEPTH, D_CONV), 0.02),
        'conv_ln_b': nrm(ks[13], (DEPTH, D_CONV), 0.02),
        'conv_pw_w': nrm(ks[14], (DEPTH, D_CONV, D_MODEL), D_CONV ** -0.5),
        'sc_dw_w': nrm(ks[15], (DEPTH, SC_K, D_SC), SC_K ** -0.5),
        'sc_out_w': nrm(ks[16], (DEPTH, D_SC, D_MODEL), D_SC ** -0.5),
        'na_rpb': nrm(ks[17], (DEPTH, NA_HEADS, RPB_H, RPB_W), 0.1),
        'na_out_w': nrm(ks[18], (DEPTH, D_ATTN, D_MODEL), D_ATTN ** -0.5),
        'w_o': nrm(ks[19], (DEPTH, D_MODEL, D_MODEL), f),
        'norm2_g': 1.0 + nrm(ks[20], (DEPTH, D_MODEL), 0.02),
        'ffn_w_gate': nrm(ks[21], (DEPTH, D_MODEL, D_FF), f),
        'ffn_w_up': nrm(ks[22], (DEPTH, D_MODEL, D_FF), f),
        'ffn_w_down': nrm(ks[23], (DEPTH, D_FF, D_MODEL), D_FF ** -0.5),
        'final_g': 1.0 + nrm(ks[24], (D_MODEL,), 0.02),
    }


def reference(x_prompt, x_sample, cache_k, cache_v, c, c_ctx, w_ada, b_ada, norm1_g, w_in,
              conv_dw_w, conv_dw_b, conv_ln_g, conv_ln_b, conv_pw_w, sc_dw_w, sc_out_w,
              na_rpb, na_out_w, w_o, norm2_g, ffn_w_gate, ffn_w_up, ffn_w_down, final_g):
    cond_ctx = c_ctx[None, None, :]
    cond_lat = c[:, None, :]
    xp = x_prompt
    xs = x_sample
    new_k_list = []
    new_v_list = []
    for l in range(DEPTH):
        p = {
            'w_ada': w_ada[l], 'b_ada': b_ada[l], 'norm1_g': norm1_g[l], 'w_in': w_in[l],
            'conv_dw_w': conv_dw_w[l], 'conv_dw_b': conv_dw_b[l], 'conv_ln_g': conv_ln_g[l],
            'conv_ln_b': conv_ln_b[l], 'conv_pw_w': conv_pw_w[l], 'sc_dw_w': sc_dw_w[l],
            'sc_out_w': sc_out_w[l], 'na_rpb': na_rpb[l], 'na_out_w': na_out_w[l], 'w_o': w_o[l],
            'norm2_g': norm2_g[l], 'ffn_w_gate': ffn_w_gate[l], 'ffn_w_up': ffn_w_up[l],
            'ffn_w_down': ffn_w_down[l],
        }
        xp, k_l, v_l = trunk_layer(xp, cond_ctx, p, attend_context)
        new_k_list.append(k_l)
        new_v_list.append(v_l)
        attend_l = functools.partial(attend_latent, k_ctx=cache_k[:, l], v_ctx=cache_v[:, l])
        xs, _, _ = trunk_layer(xs, cond_lat, p, attend_l)
    y_prompt = rmsnorm(xp, final_g)
    y_sample = rmsnorm(xs, final_g)
    new_k = jnp.stack(new_k_list, axis=1)
    new_v = jnp.stack(new_v_list, axis=1)
    return (y_prompt, y_sample, new_k, new_v)
```

```python
import functools

import numpy as np
import jax
import jax.numpy as jnp
from jax import lax
from jax.experimental import pallas as pl
from jax.experimental.pallas import tpu as pltpu

F32 = jnp.float32
BF16 = jnp.bfloat16

D_MODEL = 1024
HEADS = 16
HEAD_DIM = 64
GRID_W = 64
WIN_H = 8
WIN_W = 16
CONV_K = 31
SC_K = 3
EPS = 1e-6
LANES = 128
HALO = 16
MASK_VALUE = -1e30

TM_MM = 1024
TM_MERGE = 512
TM_CONV = 256
TN_PROJ = 512
NA_QROWS = 4
NA_KROWS = 12
VMEM_LIMIT = 48 * 1024 * 1024


def _dot(a, b):
    return jnp.dot(a, b, preferred_element_type=F32)


def _dot_nt(a, b):
    return lax.dot_general(a, b, (((1,), (1,)), ((), ())), preferred_element_type=F32)


def _silu(x):
    return x * jax.nn.sigmoid(x)


def _norm_mod(x, g, scale, shift):
    xn = x * lax.rsqrt(jnp.mean(x * x, axis=-1, keepdims=True) + EPS)
    return (xn * g) * (1.0 + scale) + shift


def _params(*sem):
    return pltpu.CompilerParams(dimension_semantics=sem, vmem_limit_bytes=VMEM_LIMIT)


def _adaln_kernel(cond_ref, w_ref, b_ref, o_ref):
    a = _silu(cond_ref[...]).astype(BF16)
    o_ref[...] = _dot(a, w_ref[...].astype(BF16)) + b_ref[...]


def _adaln(cond, w_ada, b_ada):
    depth, d, n = w_ada.shape
    rows = cond.shape[0]
    tn = 1536
    return pl.pallas_call(
        _adaln_kernel,
        grid=(depth, n // tn),
        in_specs=[pl.BlockSpec((rows, d), lambda l, j: (0, 0)),
                  pl.BlockSpec((None, d, tn), lambda l, j: (l, 0, j)),
                  pl.BlockSpec((None, 1, tn), lambda l, j: (l, 0, j))],
        out_specs=pl.BlockSpec((None, rows, tn), lambda l, j: (l, 0, j)),
        out_shape=jax.ShapeDtypeStruct((depth, rows, n), F32),
        compiler_params=_params("parallel", "parallel"),
        name="adaln",
    )(cond, w_ada, b_ada.reshape(depth, 1, n))


def _prenorm_kernel(x_ref, g_ref, mod_ref, u_ref):
    m = mod_ref[0]
    u_ref[...] = _norm_mod(x_ref[...], g_ref[...], m[1:2], m[0:1]).astype(BF16)


def _prenorm(x, g, mods, mod_row):
    n, d = x.shape
    return pl.pallas_call(
        _prenorm_kernel,
        grid=(n // TM_MM,),
        in_specs=[pl.BlockSpec((TM_MM, d), lambda i: (i, 0)),
                  pl.BlockSpec((1, d), lambda i: (0, 0)),
                  pl.BlockSpec((1, 6, d), lambda i: (mod_row(i, TM_MM), 0, 0))],
        out_specs=pl.BlockSpec((TM_MM, d), lambda i: (i, 0)),
        out_shape=jax.ShapeDtypeStruct((n, d), BF16),
        compiler_params=_params("parallel"),
        name="prenorm",
    )(x, g, mods)


def _proj_glu_kernel(u_ref, wa_ref, wg_ref, o_ref):
    u = u_ref[...]
    o_ref[...] = (_dot(u, wa_ref[...]) * jax.nn.sigmoid(_dot(u, wg_ref[...]))).astype(BF16)


def _proj_sc_kernel(u_ref, wb_ref, wc_ref, wx_ref, t_ref, bg_ref):
    u = u_ref[...]
    t_ref[...] = (_dot(u, wc_ref[...]) * _dot(u, wx_ref[...])).astype(BF16)
    bg_ref[...] = _dot(u, wb_ref[...]).astype(BF16)


def _proj_qkv_kernel(u_ref, wq_ref, wk_ref, wv_ref, q_ref, k_ref, v_ref):
    u = u_ref[...]
    q_ref[...] = (_dot(u, wq_ref[...]) * (HEAD_DIM ** -0.5)).astype(BF16)
    k_ref[...] = _dot(u, wk_ref[...]).astype(BF16)
    v_ref[...] = _dot(u, wv_ref[...]).astype(BF16)


def _proj_kv_kernel(u_ref, w_ref, o_ref):
    o_ref[...] = _dot(u_ref[...], w_ref[...])


def _w_spec(l, col_off):
    off = col_off // TN_PROJ
    return pl.BlockSpec((None, D_MODEL, TN_PROJ), lambda i, j: (l, 0, j + off))


def _proj_call(kernel, u, w_in, l, col_offs, n_out, out_dtype, name, row_off=0, rows=None):
    n = u.shape[0] if rows is None else rows
    width = D_MODEL if out_dtype == BF16 else 2 * D_MODEL
    roff = row_off // TM_MM
    out = [jax.ShapeDtypeStruct((n, width), out_dtype)] * n_out
    out_specs = [pl.BlockSpec((TM_MM, TN_PROJ), lambda i, j: (i, j))] * n_out
    res = pl.pallas_call(
        kernel,
        grid=(n // TM_MM, width // TN_PROJ),
        in_specs=[pl.BlockSpec((TM_MM, D_MODEL), lambda i, j: (i + roff, 0))]
                 + [_w_spec(l, c) for c in col_offs],
        out_specs=out_specs if n_out > 1 else out_specs[0],
        out_shape=out if n_out > 1 else out[0],
        compiler_params=_params("parallel", "arbitrary"),
        name=name,
    )(u, *([w_in] * len(col_offs)))
    return res


def _fill_window(win_ref, prev_ref, cur_ref, next_ref, first, last, tm):
    prev = prev_ref[...].astype(F32)
    nxt = next_ref[...].astype(F32)
    win_ref[0:HALO, :] = jnp.where(first, 0.0, prev)
    win_ref[HALO:HALO + tm, :] = cur_ref[...].astype(F32)
    win_ref[HALO + tm:HALO + tm + HALO, :] = jnp.where(last, 0.0, nxt)


def _depthwise(win_ref, w_ref, out_ref, taps, tm):
    base = HALO - (taps - 1) // 2
    rows = 64

    def body(cb, carry):
        col = pl.multiple_of(cb * LANES, LANES)
        for r0 in range(0, tm, rows):
            acc = None
            for k in range(taps):
                term = win_ref[pl.ds(base + r0 + k, rows), pl.ds(col, LANES)] * w_ref[pl.ds(k, 1), pl.ds(col, LANES)]
                acc = term if acc is None else acc + term
            out_ref[pl.ds(r0, rows), pl.ds(col, LANES)] = acc
        return carry

    lax.fori_loop(0, D_MODEL // LANES, body, 0)


def _conv_kernel(hp_ref, hc_ref, hn_ref, tp_ref, tc_ref, tn_ref, bg_ref,
                 wdw_ref, bdw_ref, lng_ref, lnb_ref, wsc_ref,
                 h1_ref, s_ref, win_ref, acc_ref, *, lat_tiles, lat_tiles_per_seq):
    tm = TM_CONV
    i = pl.program_id(0)
    is_lat = i < lat_tiles
    pos = i % lat_tiles_per_seq
    first = jnp.logical_or(jnp.logical_not(is_lat), pos == 0)
    last = jnp.logical_or(jnp.logical_not(is_lat), pos == lat_tiles_per_seq - 1)

    _fill_window(win_ref, hp_ref, hc_ref, hn_ref, first, last, tm)
    _depthwise(win_ref, wdw_ref, acc_ref, CONV_K, tm)
    y = acc_ref[...] + bdw_ref[...]
    mu = jnp.mean(y, axis=-1, keepdims=True)
    dlt = y - mu
    var = jnp.mean(dlt * dlt, axis=-1, keepdims=True)
    h = (dlt * lax.rsqrt(var + EPS)) * lng_ref[...] + lnb_ref[...]
    h1_ref[...] = _silu(h).astype(BF16)

    _fill_window(win_ref, tp_ref, tc_ref, tn_ref, first, last, tm)
    _depthwise(win_ref, wsc_ref, acc_ref, SC_K, tm)
    s_ref[...] = (bg_ref[...].astype(F32) * acc_ref[...]).astype(BF16)


def _conv_mixers(h0, t, bg, wdw, bdw, lng, lnb, wsc, lat_tokens, lat_seq):
    n, d = h0.shape
    tm = TM_CONV
    per = tm // HALO
    nblk = n // HALO
    cur = pl.BlockSpec((tm, d), lambda i: (i, 0))
    prev = pl.BlockSpec((HALO, d), lambda i: (jnp.maximum(i * per - 1, 0), 0))
    nxt = pl.BlockSpec((HALO, d), lambda i: (jnp.minimum((i + 1) * per, nblk - 1), 0))
    vec = pl.BlockSpec((1, d), lambda i: (0, 0))
    kern = functools.partial(_conv_kernel, lat_tiles=lat_tokens // tm, lat_tiles_per_seq=lat_seq // tm)
    return pl.pallas_call(
        kern,
        grid=(n // tm,),
        in_specs=[prev, cur, nxt, prev, cur, nxt, cur,
                  pl.BlockSpec((CONV_K, d), lambda i: (0, 0)), vec, vec, vec,
                  pl.BlockSpec((SC_K, d), lambda i: (0, 0))],
        out_specs=[cur, cur],
        out_shape=[jax.ShapeDtypeStruct((n, d), BF16)] * 2,
        scratch_shapes=[pltpu.VMEM((tm + 2 * HALO, d), F32), pltpu.VMEM((tm, d), F32)],
        compiler_params=_params("parallel"),
        name="conv_mixers",
    )(h0, h0, h0, t, t, t, bg, wdw, bdw, lng, lnb, wsc)


def _attn_ctx_kernel(q_ref, k_ref, v_ref, o_ref):
    outs = []
    for h in range(LANES // HEAD_DIM):
        sl = slice(h * HEAD_DIM, (h + 1) * HEAD_DIM)
        s = _dot_nt(q_ref[:, sl], k_ref[:, sl])
        e = jnp.exp(s - jnp.max(s, axis=-1, keepdims=True))
        den = jnp.sum(e, axis=-1, keepdims=True)
        outs.append(_dot(e.astype(BF16), v_ref[:, sl]) / den)
    o_ref[...] = jnp.concatenate(outs, axis=-1).astype(BF16)


def _attn_ctx(q, k, v, lat_tokens, batch, seq):
    roff = lat_tokens // seq
    spec = pl.BlockSpec((seq, LANES), lambda b, hp: (b + roff, hp))
    return pl.pallas_call(
        _attn_ctx_kernel,
        grid=(batch, D_MODEL // LANES),
        in_specs=[spec, spec, spec],
        out_specs=pl.BlockSpec((seq, LANES), lambda b, hp: (b, hp)),
        out_shape=jax.ShapeDtypeStruct((batch * seq, D_MODEL), BF16),
        compiler_params=_params("parallel", "parallel"),
        name="attn_ctx",
    )(q, k, v)


def _na_window_start(rb, grid_rows):
    return jnp.clip(rb * NA_QROWS - WIN_H // 2, 0, grid_rows - NA_KROWS)


def _attn_lat_kernel(q_ref, k_ref, v_ref, kc_ref, vc_ref, bias_ref, o_ref, *, grid_rows):
    rb = pl.program_id(2)
    start = pl.multiple_of(_na_window_start(rb, grid_rows) * GRID_W, GRID_W)
    nk = NA_KROWS * GRID_W
    outs = []
    for h in range(LANES // HEAD_DIM):
        sl = slice(h * HEAD_DIM, (h + 1) * HEAD_DIM)
        q = q_ref[:, sl]
        s_loc = _dot_nt(q, k_ref[pl.ds(start, nk), sl]) + bias_ref[h]
        s_ctx = _dot_nt(q, kc_ref[h])
        m = jnp.maximum(jnp.max(s_loc, axis=-1, keepdims=True), jnp.max(s_ctx, axis=-1, keepdims=True))
        e_loc = jnp.exp(s_loc - m)
        e_ctx = jnp.exp(s_ctx - m)
        den = jnp.sum(e_loc, axis=-1, keepdims=True) + jnp.sum(e_ctx, axis=-1, keepdims=True)
        o = _dot(e_loc.astype(BF16), v_ref[pl.ds(start, nk), sl]) + _dot(e_ctx.astype(BF16), vc_ref[h])
        outs.append(o / den)
    o_ref[...] = jnp.concatenate(outs, axis=-1).astype(BF16)


def _attn_lat(q, k, v, kc, vc, bias, l, dec_batch, dec_seq):
    grid_rows = dec_seq // GRID_W
    nrb = grid_rows // NA_QROWS
    tq = NA_QROWS * GRID_W
    hp_n = LANES // HEAD_DIM
    past = kc.shape[3]

    def bias_type(rb):
        return jnp.where(rb == 0, 0, jnp.where(rb == nrb - 1, 2, 1))

    seq_spec = pl.BlockSpec((dec_seq, LANES), lambda b, hp, rb: (b, hp))
    ctx_spec = pl.BlockSpec((None, None, hp_n, past, HEAD_DIM), lambda b, hp, rb: (b, l, hp, 0, 0))
    q_spec = pl.BlockSpec((tq, LANES), lambda b, hp, rb: (b * nrb + rb, hp))
    return pl.pallas_call(
        functools.partial(_attn_lat_kernel, grid_rows=grid_rows),
        grid=(dec_batch, D_MODEL // LANES, nrb),
        in_specs=[q_spec, seq_spec, seq_spec, ctx_spec, ctx_spec,
                  pl.BlockSpec((hp_n, None, tq, NA_KROWS * GRID_W), lambda b, hp, rb: (hp, bias_type(rb), 0, 0))],
        out_specs=q_spec,
        out_shape=jax.ShapeDtypeStruct((dec_batch * dec_seq, D_MODEL), BF16),
        compiler_params=_params("parallel", "parallel", "arbitrary"),
        name="attn_lat",
    )(q, k, v, kc, vc, bias)


def _na_bias_tables(rpb, grid_rows):
    nrb = grid_rows // NA_QROWS
    c = np.arange(GRID_W)
    cs = np.clip(c - WIN_W // 2, 0, GRID_W - WIN_W)
    kc = np.arange(GRID_W)
    col_ok = (kc[None, :] >= cs[:, None]) & (kc[None, :] < cs[:, None] + WIN_W)
    dcol = np.clip(kc[None, :] - c[:, None] + WIN_W - 1, 0, 2 * WIN_W - 2)
    row_idx = np.zeros((3, NA_QROWS, NA_KROWS), np.int32)
    row_ok = np.zeros((3, NA_QROWS, NA_KROWS), bool)
    for typ, rb in enumerate((0, 1, nrb - 1)):
        ws = int(np.clip(rb * NA_QROWS - WIN_H // 2, 0, grid_rows - NA_KROWS))
        r = rb * NA_QROWS + np.arange(NA_QROWS)
        rs = np.clip(r - WIN_H // 2, 0, grid_rows - WIN_H)
        kr = ws + np.arange(NA_KROWS)
        row_ok[typ] = (kr[None, :] >= rs[:, None]) & (kr[None, :] < rs[:, None] + WIN_H)
        row_idx[typ] = np.clip(kr[None, :] - r[:, None] + WIN_H - 1, 0, 2 * WIN_H - 2)
    toep = jnp.where(col_ok[None, None, None], rpb[:, :, :, dcol], MASK_VALUE)
    blocks = toep[:, :, row_idx]
    blocks = jnp.where(row_ok[None, None, :, :, :, None, None], blocks, MASK_VALUE)
    blocks = jnp.transpose(blocks, (0, 1, 2, 3, 5, 4, 6))
    depth, heads = rpb.shape[0], rpb.shape[1]
    return blocks.reshape(depth, heads, 3, NA_QROWS * GRID_W, NA_KROWS * GRID_W)


def _merge_kernel(x_ref, u_ref, h1_ref, ya_ref, s_ref, w3_ref, wg_ref, wo_ref, mod_ref, g2_ref,
                  x1_ref, u2_ref, acc_ref):
    j = pl.program_id(1)
    gate = jax.nn.sigmoid(_dot(u_ref[...], wg_ref[...]))

    @pl.when(j == 0)
    def _():
        acc_ref[...] = gate * _dot(h1_ref[...], w3_ref[...])

    @pl.when(j == 1)
    def _():
        acc_ref[...] += gate * _dot(ya_ref[...], w3_ref[...])

    @pl.when(j == 2)
    def _():
        mixed = acc_ref[...] + gate * _dot(s_ref[...], w3_ref[...])
        m = mod_ref[0]
        x1 = x_ref[...] + m[2:3] * _dot(mixed.astype(BF16), wo_ref[...])
        x1_ref[...] = x1
        u2_ref[...] = _norm_mod(x1, g2_ref[...], m[4:5], m[3:4]).astype(BF16)


def _merge(x, u, h1, ya, s, w3, w_in, w_o, mods, g2, l, mod_row):
    n, d = x.shape
    tm = TM_MERGE
    gate_off = (w_in.shape[2] - 3 * d) // d
    tile = pl.BlockSpec((tm, d), lambda i, j: (i, 0))
    return pl.pallas_call(
        _merge_kernel,
        grid=(n // tm, 3),
        in_specs=[tile, tile, tile, tile, tile,
                  pl.BlockSpec((None, d, d), lambda i, j: (j, 0, 0)),
                  pl.BlockSpec((None, d, d), lambda i, j: (l, 0, j + gate_off)),
                  pl.BlockSpec((None, d, d), lambda i, j: (l, 0, 0)),
                  pl.BlockSpec((1, 6, d), lambda i, j: (mod_row(i, tm), 0, 0)),
                  pl.BlockSpec((1, d), lambda i, j: (0, 0))],
        out_specs=[tile, tile],
        out_shape=[jax.ShapeDtypeStruct((n, d), F32), jax.ShapeDtypeStruct((n, d), BF16)],
        scratch_shapes=[pltpu.VMEM((tm, d), F32)],
        compiler_params=_params("parallel", "arbitrary"),
        name="merge",
    )(x, u, h1, ya, s, w3, w_in, w_o, mods, g2)


def _ffn_kernel(x_ref, u_ref, wg_ref, wu_ref, wd_ref, mod_ref, gn_ref, modn_ref, *rest, final):
    if final:
        y_ref, acc_ref = rest
    else:
        x2_ref, un_ref, acc_ref = rest
    j = pl.program_id(1)
    u = u_ref[...]
    hidden = _silu(_dot(u, wg_ref[...])) * _dot(u, wu_ref[...])
    part = _dot(hidden.astype(BF16), wd_ref[...])

    @pl.when(j == 0)
    def _():
        acc_ref[...] = part

    @pl.when(j == pl.num_programs(1) - 1)
    def _():
        x2 = x_ref[...] + mod_ref[0][5:6] * (acc_ref[...] + part)
        if final:
            xn = x2 * lax.rsqrt(jnp.mean(x2 * x2, axis=-1, keepdims=True) + EPS)
            y_ref[...] = xn * gn_ref[...]
        else:
            mn = modn_ref[0]
            x2_ref[...] = x2
            un_ref[...] = _norm_mod(x2, gn_ref[...], mn[1:2], mn[0:1]).astype(BF16)


def _ffn(x, u, wg, wu, wd, mods, g_next, mods_next, l, mod_row, final):
    n, d = x.shape
    tm = TM_MERGE
    dff = wg.shape[2]
    chunks = 2
    tf = dff // chunks
    tile = pl.BlockSpec((tm, d), lambda i, j: (i, 0))
    modspec = pl.BlockSpec((1, 6, d), lambda i, j: (mod_row(i, tm), 0, 0))
    if final:
        out_shape = jax.ShapeDtypeStruct((n, d), F32)
        out_specs = tile
    else:
        out_shape = [jax.ShapeDtypeStruct((n, d), F32), jax.ShapeDtypeStruct((n, d), BF16)]
        out_specs = [tile, tile]
    return pl.pallas_call(
        functools.partial(_ffn_kernel, final=final),
        grid=(n // tm, chunks),
        in_specs=[tile, tile,
                  pl.BlockSpec((None, d, tf), lambda i, j: (l, 0, j)),
                  pl.BlockSpec((None, d, tf), lambda i, j: (l, 0, j)),
                  pl.BlockSpec((None, tf, d), lambda i, j: (l, j, 0)),
                  modspec,
                  pl.BlockSpec((1, d), lambda i, j: (0, 0)),
                  modspec],
        out_specs=out_specs,
        out_shape=out_shape,
        scratch_shapes=[pltpu.VMEM((tm, d), F32)],
        compiler_params=_params("parallel", "arbitrary"),
        name="ffn_final" if final else "ffn",
    )(x, u, wg, wu, wd, mods, g_next, mods_next)


def kernel(x_prompt, x_sample, cache_k, cache_v, c, c_ctx, w_ada, b_ada, norm1_g, w_in, conv_dw_w, conv_dw_b, conv_ln_g, conv_ln_b, conv_pw_w, sc_dw_w, sc_out_w, na_rpb, na_out_w, w_o, norm2_g, ffn_w_gate, ffn_w_up, ffn_w_down, final_g):
    batch, seq, d = x_prompt.shape
    dec_batch, dec_seq, _ = x_sample.shape
    depth = w_ada.shape[0]
    n_lat = dec_batch * dec_seq
    n_ctx = batch * seq
    grid_rows = dec_seq // GRID_W
    assert d == D_MODEL and w_in.shape[2] == 11 * d
    assert seq == TM_CONV and dec_seq % TM_MM == 0 and n_ctx % TM_MM == 0
    assert grid_rows % NA_QROWS == 0 and grid_rows >= NA_KROWS

    def mod_row(i, tm):
        return jnp.minimum((i * tm) // dec_seq, dec_batch)

    n_cond = -(-(dec_batch + 1) // 8) * 8
    cond = jnp.zeros((n_cond, d), F32).at[:dec_batch].set(c).at[dec_batch].set(c_ctx)
    mods = _adaln(cond, w_ada, b_ada).reshape(depth, n_cond, 6, d)

    w_in_b = w_in.astype(BF16)
    w3_b = jnp.stack([conv_pw_w, na_out_w, sc_out_w], axis=1).astype(BF16)
    w_o_b = w_o.astype(BF16)
    wg_b, wu_b, wd_b = ffn_w_gate.astype(BF16), ffn_w_up.astype(BF16), ffn_w_down.astype(BF16)
    kc_b, vc_b = cache_k.astype(BF16), cache_v.astype(BF16)
    bias = _na_bias_tables(na_rpb, grid_rows)

    x = jnp.concatenate([x_sample.reshape(n_lat, d), x_prompt.reshape(n_ctx, d)], axis=0)
    u = _prenorm(x, norm1_g[0][None], mods[0], mod_row)
    kv_ctx = []
    for l in range(depth):
        h0 = _proj_call(_proj_glu_kernel, u, w_in_b, l, (0, d), 1, BF16, "proj_glu")
        q, k, v = _proj_call(_proj_qkv_kernel, u, w_in_b, l, (2 * d, 3 * d, 4 * d), 3, BF16, "proj_qkv")
        kv_ctx.append(_proj_call(_proj_kv_kernel, u, w_in_b, l, (3 * d,), 1, F32, "proj_kv_ctx",
                                 row_off=n_lat, rows=n_ctx))
        t, bg = _proj_call(_proj_sc_kernel, u, w_in_b, l, (5 * d, 6 * d, 7 * d), 2, BF16, "proj_sc")
        h1, s = _conv_mixers(h0, t, bg, conv_dw_w[l], conv_dw_b[l][None], conv_ln_g[l][None],
                             conv_ln_b[l][None], sc_dw_w[l], n_lat, dec_seq)
        ya_lat = _attn_lat(q, k, v, kc_b, vc_b, bias[l], l, dec_batch, dec_seq)
        ya_ctx = _attn_ctx(q, k, v, n_lat, batch, seq)
        ya = jnp.concatenate([ya_lat, ya_ctx], axis=0)
        x, u2 = _merge(x, u, h1, ya, s, w3_b[l], w_in_b, w_o_b, mods[l], norm2_g[l][None], l, mod_row)
        if l + 1 < depth:
            x, u = _ffn(x, u2, wg_b, wu_b, wd_b, mods[l], norm1_g[l + 1][None], mods[l + 1], l, mod_row, False)
        else:
            y = _ffn(x, u2, wg_b, wu_b, wd_b, mods[l], final_g[None], mods[l], l, mod_row, True)

    y_sample = y[:n_lat].reshape(dec_batch, dec_seq, d)
    y_prompt = y[n_lat:].reshape(batch, seq, d)
    kv = jnp.stack(kv_ctx, axis=0).reshape(depth, batch, seq, 2, HEADS, HEAD_DIM)
    kv = jnp.transpose(kv, (3, 1, 0, 4, 2, 5))
    return (y_prompt, y_sample, kv[0], kv[1])
```

```python
import functools

import numpy as np
import jax
import jax.numpy as jnp
from jax import lax
from jax.experimental import pallas as pl
from jax.experimental.pallas import tpu as pltpu

F32 = jnp.float32
BF16 = jnp.bfloat16

D_MODEL = 1024
HEADS = 16
HEAD_DIM = 64
GRID_W = 64
WIN_H = 8
WIN_W = 16
CONV_K = 31
SC_K = 3
EPS = 1e-6
LANES = 128
SUBLANES = 8
HALO = 16
MASK_VALUE = -1e30

TM_MM = 1024
TM_MERGE = 512
TM_CONV = 256
TN_PROJ = 512
NA_QROWS = 8
CTX_LANES = 512
VMEM_LIMIT = 48 * 1024 * 1024


def _dot(a, b):
    return jnp.dot(a, b, preferred_element_type=F32)


def _dot_nt(a, b):
    return lax.dot_general(a, b, (((1,), (1,)), ((), ())), preferred_element_type=F32)


def _silu(x):
    return x * jax.nn.sigmoid(x)


def _norm_mod(x, g, scale, shift):
    xn = x * lax.rsqrt(jnp.mean(x * x, axis=-1, keepdims=True) + EPS)
    return (xn * g) * (1.0 + scale) + shift


def _params(*sem):
    return pltpu.CompilerParams(dimension_semantics=sem, vmem_limit_bytes=VMEM_LIMIT)


def _adaln_kernel(cond_ref, w_ref, b_ref, o_ref):
    a = _silu(cond_ref[...]).astype(BF16)
    o_ref[...] = _dot(a, w_ref[...].astype(BF16)) + b_ref[...]


def _adaln(cond, w_ada, b_ada):
    depth, d, n = w_ada.shape
    rows = cond.shape[0]
    tn = 1536
    return pl.pallas_call(
        _adaln_kernel,
        grid=(depth, n // tn),
        in_specs=[pl.BlockSpec((rows, d), lambda l, j: (0, 0)),
                  pl.BlockSpec((None, d, tn), lambda l, j: (l, 0, j)),
                  pl.BlockSpec((None, 1, tn), lambda l, j: (l, 0, j))],
        out_specs=pl.BlockSpec((None, rows, tn), lambda l, j: (l, 0, j)),
        out_shape=jax.ShapeDtypeStruct((depth, rows, n), F32),
        compiler_params=_params("parallel", "parallel"),
        name="adaln",
    )(cond, w_ada, b_ada.reshape(depth, 1, n))


def _prenorm_kernel(x_ref, g_ref, mod_ref, u_ref):
    m = mod_ref[0]
    u_ref[...] = _norm_mod(x_ref[...], g_ref[...], m[1:2], m[0:1]).astype(BF16)


def _prenorm(x, g, mods, mod_row):
    n, d = x.shape
    return pl.pallas_call(
        _prenorm_kernel,
        grid=(n // TM_MM,),
        in_specs=[pl.BlockSpec((TM_MM, d), lambda i: (i, 0)),
                  pl.BlockSpec((1, d), lambda i: (0, 0)),
                  pl.BlockSpec((1, 6, d), lambda i: (mod_row(i, TM_MM), 0, 0))],
        out_specs=pl.BlockSpec((TM_MM, d), lambda i: (i, 0)),
        out_shape=jax.ShapeDtypeStruct((n, d), BF16),
        compiler_params=_params("parallel"),
        name="prenorm",
    )(x, g, mods)


def _proj_glu_kernel(u_ref, wa_ref, wg_ref, o_ref):
    u = u_ref[...]
    o_ref[...] = (_dot(u, wa_ref[...]) * jax.nn.sigmoid(_dot(u, wg_ref[...]))).astype(BF16)


def _proj_sc_kernel(u_ref, wb_ref, wc_ref, wx_ref, t_ref, bg_ref):
    u = u_ref[...]
    t_ref[...] = (_dot(u, wc_ref[...]) * _dot(u, wx_ref[...])).astype(BF16)
    bg_ref[...] = _dot(u, wb_ref[...]).astype(BF16)


def _proj_qkv_kernel(u_ref, wq_ref, wk_ref, wv_ref, q_ref, k_ref, v_ref):
    u = u_ref[...]
    q_ref[...] = (_dot(u, wq_ref[...]) * (HEAD_DIM ** -0.5)).astype(BF16)
    k_ref[...] = _dot(u, wk_ref[...]).astype(BF16)
    v_ref[...] = _dot(u, wv_ref[...]).astype(BF16)


def _w_spec(l, col_off):
    off = col_off // TN_PROJ
    return pl.BlockSpec((None, D_MODEL, TN_PROJ), lambda i, j: (l, 0, j + off))


def _proj_call(kernel, u, w_in, l, col_offs, n_out, name):
    n = u.shape[0]
    out = [jax.ShapeDtypeStruct((n, D_MODEL), BF16)] * n_out
    out_specs = [pl.BlockSpec((TM_MM, TN_PROJ), lambda i, j: (i, j))] * n_out
    res = pl.pallas_call(
        kernel,
        grid=(n // TM_MM, D_MODEL // TN_PROJ),
        in_specs=[pl.BlockSpec((TM_MM, D_MODEL), lambda i, j: (i, 0))]
                 + [_w_spec(l, c) for c in col_offs],
        out_specs=out_specs if n_out > 1 else out_specs[0],
        out_shape=out if n_out > 1 else out[0],
        compiler_params=_params("parallel", "arbitrary"),
        name=name,
    )(u, *([w_in] * len(col_offs)))
    return res


def _proj_kv_ctx_kernel(u_ref, wk_ref, wv_ref, *rest):
    k_ref, v_ref = rest[-2:]
    u = u_ref[...]
    rk = _dot(u, wk_ref[...])
    rv = _dot(u, wv_ref[...])
    for h in range(TN_PROJ // HEAD_DIM):
        sl = slice(h * HEAD_DIM, (h + 1) * HEAD_DIM)
        k_ref[h] = rk[:, sl]
        v_ref[h] = rv[:, sl]


def _proj_kv_ctx(u, w_in, l, n_lat, batch, seq, depth, prev):
    heads_per = TN_PROJ // HEAD_DIM
    roff = n_lat // seq
    koff, voff = 3 * D_MODEL // TN_PROJ, 4 * D_MODEL // TN_PROJ
    out_spec = pl.BlockSpec((None, None, heads_per, seq, HEAD_DIM), lambda b, j: (b, l, j, 0, 0))
    out_sds = jax.ShapeDtypeStruct((batch, depth, HEADS, seq, HEAD_DIM), F32)
    in_specs = [pl.BlockSpec((seq, D_MODEL), lambda b, j: (b + roff, 0)),
                pl.BlockSpec((None, D_MODEL, TN_PROJ), lambda b, j: (l, 0, j + koff)),
                pl.BlockSpec((None, D_MODEL, TN_PROJ), lambda b, j: (l, 0, j + voff))]
    args = [u, w_in, w_in]
    aliases = {}
    if prev is not None:
        in_specs += [pl.BlockSpec(memory_space=pl.ANY)] * 2
        args += list(prev)
        aliases = {3: 0, 4: 1}
    return pl.pallas_call(
        _proj_kv_ctx_kernel,
        grid=(batch, D_MODEL // TN_PROJ),
        in_specs=in_specs,
        out_specs=[out_spec, out_spec],
        out_shape=[out_sds, out_sds],
        input_output_aliases=aliases,
        compiler_params=_params("parallel", "arbitrary"),
        name="proj_kv_ctx",
    )(*args)


def _fill_window(win_ref, prev_ref, cur_ref, next_ref, first, last, tm):
    prev = prev_ref[...].astype(F32)
    nxt = next_ref[...].astype(F32)
    win_ref[0:HALO, :] = jnp.where(first, 0.0, prev)
    win_ref[HALO:HALO + tm, :] = cur_ref[...].astype(F32)
    win_ref[HALO + tm:HALO + tm + HALO, :] = jnp.where(last, 0.0, nxt)


def _depthwise(win_ref, sh_ref, w_ref, out_ref, taps, tm):
    base = HALO - (taps - 1) // 2
    rows = 64
    sh_rows = sh_ref.shape[1]
    shifts = sorted({(base + k) % SUBLANES for k in range(taps)} - {0})

    def body(cb, carry):
        col = pl.multiple_of(cb * LANES, LANES)
        for s in shifts:
            sh_ref[s] = win_ref[pl.ds(s, sh_rows), pl.ds(col, LANES)]
        for r0 in range(0, tm, rows):
            acc = None
            for k in range(taps):
                off = base + r0 + k
                s, aligned = off % SUBLANES, off - off % SUBLANES
                if s == 0:
                    xk = win_ref[pl.ds(aligned, rows), pl.ds(col, LANES)]
                else:
                    xk = sh_ref[s, pl.ds(aligned, rows), :]
                term = xk * w_ref[pl.ds(k, 1), pl.ds(col, LANES)]
                acc = term if acc is None else acc + term
            out_ref[pl.ds(r0, rows), pl.ds(col, LANES)] = acc
        return carry

    lax.fori_loop(0, D_MODEL // LANES, body, 0)


def _conv_kernel(hp_ref, hc_ref, hn_ref, tp_ref, tc_ref, tn_ref, bg_ref,
                 wdw_ref, bdw_ref, lng_ref, lnb_ref, wsc_ref,
                 h1_ref, s_ref, win_ref, sh_ref, acc_ref, *, lat_tiles, lat_tiles_per_seq):
    tm = TM_CONV
    i = pl.program_id(0)
    is_lat = i < lat_tiles
    pos = i % lat_tiles_per_seq
    first = jnp.logical_or(jnp.logical_not(is_lat), pos == 0)
    last = jnp.logical_or(jnp.logical_not(is_lat), pos == lat_tiles_per_seq - 1)

    _fill_window(win_ref, hp_ref, hc_ref, hn_ref, first, last, tm)
    _depthwise(win_ref, sh_ref, wdw_ref, acc_ref, CONV_K, tm)
    y = acc_ref[...] + bdw_ref[...]
    mu = jnp.mean(y, axis=-1, keepdims=True)
    dlt = y - mu
    var = jnp.mean(dlt * dlt, axis=-1, keepdims=True)
    h = (dlt * lax.rsqrt(var + EPS)) * lng_ref[...] + lnb_ref[...]
    h1_ref[...] = _silu(h).astype(BF16)

    _fill_window(win_ref, tp_ref, tc_ref, tn_ref, first, last, tm)
    _depthwise(win_ref, sh_ref, wsc_ref, acc_ref, SC_K, tm)
    s_ref[...] = (bg_ref[...].astype(F32) * acc_ref[...]).astype(BF16)


def _conv_mixers(h0, t, bg, wdw, bdw, lng, lnb, wsc, lat_tokens, lat_seq):
    n, d = h0.shape
    tm = TM_CONV
    per = tm // HALO
    nblk = n // HALO
    cur = pl.BlockSpec((tm, d), lambda i: (i, 0))
    prev = pl.BlockSpec((HALO, d), lambda i: (jnp.maximum(i * per - 1, 0), 0))
    nxt = pl.BlockSpec((HALO, d), lambda i: (jnp.minimum((i + 1) * per, nblk - 1), 0))
    vec = pl.BlockSpec((1, d), lambda i: (0, 0))
    kern = functools.partial(_conv_kernel, lat_tiles=lat_tokens // tm, lat_tiles_per_seq=lat_seq // tm)
    return pl.pallas_call(
        kern,
        grid=(n // tm,),
        in_specs=[prev, cur, nxt, prev, cur, nxt, cur,
                  pl.BlockSpec((CONV_K, d), lambda i: (0, 0)), vec, vec, vec,
                  pl.BlockSpec((SC_K, d), lambda i: (0, 0))],
        out_specs=[cur, cur],
        out_shape=[jax.ShapeDtypeStruct((n, d), BF16)] * 2,
        scratch_shapes=[pltpu.VMEM((tm + 2 * HALO, d), F32),
                        pltpu.VMEM((SUBLANES, tm + 2 * HALO - SUBLANES, LANES), F32),
                        pltpu.VMEM((tm, d), F32)],
        compiler_params=_params("parallel"),
        name="conv_mixers",
    )(h0, h0, h0, t, t, t, bg, wdw, bdw, lng, lnb, wsc)


def _attn_ctx_kernel(q_ref, k_ref, v_ref, ya_hbm_ref, o_ref):
    del ya_hbm_ref
    for p in range(CTX_LANES // LANES):
        outs = []
        for h in range(LANES // HEAD_DIM):
            sl = slice(p * LANES + h * HEAD_DIM, p * LANES + (h + 1) * HEAD_DIM)
            s = _dot_nt(q_ref[:, sl], k_ref[:, sl])
            e = jnp.exp(s - jnp.max(s, axis=-1, keepdims=True))
            den = jnp.sum(e, axis=-1, keepdims=True)
            outs.append(_dot(e.astype(BF16), v_ref[:, sl]) / den)
        o_ref[:, p * LANES:(p + 1) * LANES] = jnp.concatenate(outs, axis=-1).astype(BF16)


def _attn_ctx(q, k, v, ya, lat_tokens, batch, seq):
    roff = lat_tokens // seq
    spec = pl.BlockSpec((seq, CTX_LANES), lambda b, hg: (b + roff, hg))
    return pl.pallas_call(
        _attn_ctx_kernel,
        grid=(batch, D_MODEL // CTX_LANES),
        in_specs=[spec, spec, spec, pl.BlockSpec(memory_space=pl.ANY)],
        out_specs=spec,
        out_shape=jax.ShapeDtypeStruct(ya.shape, BF16),
        input_output_aliases={3: 0},
        compiler_params=_params("parallel", "parallel"),
        name="attn_ctx",
    )(q, k, v, ya)


def _attn_lat_kernel(q_ref, k_ref, v_ref, kc_ref, vc_ref, bias_ref, o_ref, ko_ref, vo_ref, *, grid_rows):
    rb = pl.program_id(2)
    nloc = WIN_H * GRID_W

    @pl.when(rb == 0)
    def _():
        ko_ref[...] = k_ref[:, HEAD_DIM:]
        vo_ref[...] = v_ref[:, HEAD_DIM:]

    starts, deltas = [], []
    for j in range(NA_QROWS):
        r = rb * NA_QROWS + j
        rs = jnp.clip(r - WIN_H // 2, 0, grid_rows - WIN_H)
        starts.append(pl.multiple_of(rs * GRID_W, GRID_W))
        deltas.append(r - rs)

    heads = range(LANES // HEAD_DIM)
    scores = []
    for h in heads:
        q = q_ref[:, h * HEAD_DIM:(h + 1) * HEAD_DIM]
        s_ctx = _dot_nt(q, kc_ref[h])
        s_rows = []
        for j in range(NA_QROWS):
            kw = k_ref[pl.ds(starts[j], nloc), :HEAD_DIM] if h == 0 else ko_ref[pl.ds(starts[j], nloc), :]
            s_rows.append(_dot_nt(q[j * GRID_W:(j + 1) * GRID_W], kw) + bias_ref[h, deltas[j]])
        scores.append((jnp.concatenate(s_rows, axis=0), s_ctx))
    probs = []
    for s_loc, s_ctx in scores:
        m = jnp.maximum(jnp.max(s_loc, axis=-1, keepdims=True), jnp.max(s_ctx, axis=-1, keepdims=True))
        e_loc = jnp.exp(s_loc - m)
        e_ctx = jnp.exp(s_ctx - m)
        den = jnp.sum(e_loc, axis=-1, keepdims=True) + jnp.sum(e_ctx, axis=-1, keepdims=True)
        probs.append((e_loc.astype(BF16), e_ctx.astype(BF16), den))
    outs = []
    for h in heads:
        p_loc, p_ctx, den = probs[h]
        o_rows = []
        for j in range(NA_QROWS):
            vw = v_ref[pl.ds(starts[j], nloc), :HEAD_DIM] if h == 0 else vo_ref[pl.ds(starts[j], nloc), :]
            o_rows.append(_dot(p_loc[j * GRID_W:(j + 1) * GRID_W], vw))
        o = jnp.concatenate(o_rows, axis=0) + _dot(p_ctx, vc_ref[h])
        outs.append(o / den)
    o_ref[...] = jnp.concatenate(outs, axis=-1).astype(BF16)


def _attn_lat(q, k, v, kc, vc, bias, l, dec_batch, dec_seq):
    grid_rows = dec_seq // GRID_W
    nrb = grid_rows // NA_QROWS
    tq = NA_QROWS * GRID_W
    hp_n = LANES // HEAD_DIM
    past = kc.shape[3]
    seq_spec = pl.BlockSpec((dec_seq, LANES), lambda hp, b, rb: (b, hp))
    ctx_spec = pl.BlockSpec((None, None, hp_n, past, HEAD_DIM), lambda hp, b, rb: (b, l, hp, 0, 0))
    q_spec = pl.BlockSpec((tq, LANES), lambda hp, b, rb: (b * nrb + rb, hp))
    return pl.pallas_call(
        functools.partial(_attn_lat_kernel, grid_rows=grid_rows),
        grid=(D_MODEL // LANES, dec_batch, nrb),
        in_specs=[q_spec, seq_spec, seq_spec, ctx_spec, ctx_spec,
                  pl.BlockSpec((None, hp_n, WIN_H, GRID_W, WIN_H * GRID_W), lambda hp, b, rb: (l, hp, 0, 0, 0))],
        out_specs=q_spec,
        out_shape=jax.ShapeDtypeStruct(q.shape, BF16),
        scratch_shapes=[pltpu.VMEM((dec_seq, HEAD_DIM), BF16)] * 2,
        compiler_params=_params("arbitrary", "arbitrary", "arbitrary"),
        name="attn_lat",
    )(q, k, v, kc, vc, bias)


def _na_bias_tables(rpb):
    c = np.arange(GRID_W)
    cs = np.clip(c - WIN_W // 2, 0, GRID_W - WIN_W)
    kc = np.arange(GRID_W)
    col_ok = (kc[None, :] >= cs[:, None]) & (kc[None, :] < cs[:, None] + WIN_W)
    dcol = np.clip(kc[None, :] - c[:, None] + WIN_W - 1, 0, 2 * WIN_W - 2)
    toep = jnp.where(col_ok[None, None, None], rpb[:, :, :, dcol], MASK_VALUE)
    depth, heads = rpb.shape[0], rpb.shape[1]
    tables = []
    for delta in range(WIN_H):
        rows = toep[:, :, WIN_H - 1 - delta:2 * WIN_H - 1 - delta]
        tables.append(jnp.transpose(rows, (0, 1, 3, 2, 4)).reshape(depth, heads, GRID_W, WIN_H * GRID_W))
    return jnp.stack(tables, axis=2)


def _merge_kernel(x_ref, u_ref, h1_ref, ya_ref, s_ref, w3_ref, wg_ref, wo_ref, mod_ref, g2_ref,
                  x1_ref, u2_ref, acc_ref):
    j = pl.program_id(1)
    gate = jax.nn.sigmoid(_dot(u_ref[...], wg_ref[...]))

    @pl.when(j == 0)
    def _():
        acc_ref[...] = gate * _dot(h1_ref[...], w3_ref[...])

    @pl.when(j == 1)
    def _():
        acc_ref[...] += gate * _dot(ya_ref[...], w3_ref[...])

    @pl.when(j == 2)
    def _():
        mixed = acc_ref[...] + gate * _dot(s_ref[...], w3_ref[...])
        m = mod_ref[0]
        x1 = x_ref[...] + m[2:3] * _dot(mixed.astype(BF16), wo_ref[...])
        x1_ref[...] = x1
        u2_ref[...] = _norm_mod(x1, g2_ref[...], m[4:5], m[3:4]).astype(BF16)


def _merge(x, u, h1, ya, s, w3, w_in, w_o, mods, g2, l, mod_row):
    n, d = x.shape
    tm = TM_MERGE
    gate_off = (w_in.shape[2] - 3 * d) // d
    tile = pl.BlockSpec((tm, d), lambda i, j: (i, 0))
    return pl.pallas_call(
        _merge_kernel,
        grid=(n // tm, 3),
        in_specs=[tile, tile, tile, tile, tile,
                  pl.BlockSpec((None, None, d, d), lambda i, j: (l, j, 0, 0)),
                  pl.BlockSpec((None, d, d), lambda i, j: (l, 0, j + gate_off)),
                  pl.BlockSpec((None, d, d), lambda i, j: (l, 0, 0)),
                  pl.BlockSpec((1, 6, d), lambda i, j: (mod_row(i, tm), 0, 0)),
                  pl.BlockSpec((1, d), lambda i, j: (0, 0))],
        out_specs=[tile, tile],
        out_shape=[jax.ShapeDtypeStruct((n, d), F32), jax.ShapeDtypeStruct((n, d), BF16)],
        scratch_shapes=[pltpu.VMEM((tm, d), F32)],
        compiler_params=_params("parallel", "arbitrary"),
        name="merge",
    )(x, u, h1, ya, s, w3, w_in, w_o, mods, g2)


def _ffn_kernel(x_ref, u_ref, wg_ref, wu_ref, wd_ref, mod_ref, gn_ref, modn_ref, *rest, final):
    if final:
        y_ref, acc_ref = rest
    else:
        x2_ref, un_ref, acc_ref = rest
    j = pl.program_id(1)
    u = u_ref[...]
    hidden = _silu(_dot(u, wg_ref[...])) * _dot(u, wu_ref[...])
    part = _dot(hidden.astype(BF16), wd_ref[...])

    @pl.when(j == 0)
    def _():
        acc_ref[...] = part

    @pl.when(j == pl.num_programs(1) - 1)
    def _():
        x2 = x_ref[...] + mod_ref[0][5:6] * (acc_ref[...] + part)
        if final:
            xn = x2 * lax.rsqrt(jnp.mean(x2 * x2, axis=-1, keepdims=True) + EPS)
            y_ref[...] = xn * gn_ref[...]
        else:
            mn = modn_ref[0]
            x2_ref[...] = x2
            un_ref[...] = _norm_mod(x2, gn_ref[...], mn[1:2], mn[0:1]).astype(BF16)


def _ffn(x, u, wg, wu, wd, mods, g_next, mods_next, l, mod_row, final, row_off=0, rows=None):
    d = x.shape[1]
    n = x.shape[0] if rows is None else rows
    tm = TM_MERGE
    dff = wg.shape[2]
    chunks = 2
    tf = dff // chunks
    toff = row_off // tm
    tile = pl.BlockSpec((tm, d), lambda i, j: (i + toff, 0))
    out_tile = pl.BlockSpec((tm, d), lambda i, j: (i, 0))
    modspec = pl.BlockSpec((1, 6, d), lambda i, j: (mod_row(i + toff, tm), 0, 0))
    if final:
        out_shape = jax.ShapeDtypeStruct((n, d), F32)
        out_specs = out_tile
    else:
        out_shape = [jax.ShapeDtypeStruct((n, d), F32), jax.ShapeDtypeStruct((n, d), BF16)]
        out_specs = [out_tile, out_tile]
    return pl.pallas_call(
        functools.partial(_ffn_kernel, final=final),
        grid=(n // tm, chunks),
        in_specs=[tile, tile,
                  pl.BlockSpec((None, d, tf), lambda i, j: (l, 0, j)),
                  pl.BlockSpec((None, d, tf), lambda i, j: (l, 0, j)),
                  pl.BlockSpec((None, tf, d), lambda i, j: (l, j, 0)),
                  modspec,
                  pl.BlockSpec((1, d), lambda i, j: (0, 0)),
                  modspec],
        out_specs=out_specs,
        out_shape=out_shape,
        scratch_shapes=[pltpu.VMEM((tm, d), F32)],
        compiler_params=_params("parallel", "arbitrary"),
        name="ffn_final" if final else "ffn",
    )(x, u, wg, wu, wd, mods, g_next, mods_next)


def kernel(x_prompt, x_sample, cache_k, cache_v, c, c_ctx, w_ada, b_ada, norm1_g, w_in, conv_dw_w, conv_dw_b, conv_ln_g, conv_ln_b, conv_pw_w, sc_dw_w, sc_out_w, na_rpb, na_out_w, w_o, norm2_g, ffn_w_gate, ffn_w_up, ffn_w_down, final_g):
    batch, seq, d = x_prompt.shape
    dec_batch, dec_seq, _ = x_sample.shape
    depth = w_ada.shape[0]
    n_lat = dec_batch * dec_seq
    n_ctx = batch * seq
    grid_rows = dec_seq // GRID_W
    assert d == D_MODEL and w_in.shape[2] == 11 * d
    assert seq == TM_CONV and dec_seq % TM_MM == 0 and n_ctx % TM_MM == 0
    assert grid_rows % NA_QROWS == 0 and grid_rows >= WIN_H and n_lat % TM_MERGE == 0

    def mod_row(i, tm):
        return jnp.minimum((i * tm) // dec_seq, dec_batch)

    n_cond = -(-(dec_batch + 1) // 8) * 8
    cond = jnp.zeros((n_cond, d), F32).at[:dec_batch].set(c).at[dec_batch].set(c_ctx)
    mods = _adaln(cond, w_ada, b_ada).reshape(depth, n_cond, 6, d)

    w_in_b = w_in.astype(BF16)
    w3_b = jnp.stack([conv_pw_w, na_out_w, sc_out_w], axis=1).astype(BF16)
    w_o_b = w_o.astype(BF16)
    wg_b, wu_b, wd_b = ffn_w_gate.astype(BF16), ffn_w_up.astype(BF16), ffn_w_down.astype(BF16)
    kc_b, vc_b = cache_k.astype(BF16), cache_v.astype(BF16)
    bias = _na_bias_tables(na_rpb)

    x = jnp.concatenate([x_sample.reshape(n_lat, d), x_prompt.reshape(n_ctx, d)], axis=0)
    u = _prenorm(x, norm1_g[0][None], mods[0], mod_row)
    new_kv = None
    for l in range(depth):
        h0 = _proj_call(_proj_glu_kernel, u, w_in_b, l, (0, d), 1, "proj_glu")
        q, k, v = _proj_call(_proj_qkv_kernel, u, w_in_b, l, (2 * d, 3 * d, 4 * d), 3, "proj_qkv")
        new_kv = _proj_kv_ctx(u, w_in_b, l, n_lat, batch, seq, depth, new_kv)
        t, bg = _proj_call(_proj_sc_kernel, u, w_in_b, l, (5 * d, 6 * d, 7 * d), 2, "proj_sc")
        h1, s = _conv_mixers(h0, t, bg, conv_dw_w[l], conv_dw_b[l][None], conv_ln_g[l][None],
                             conv_ln_b[l][None], sc_dw_w[l], n_lat, dec_seq)
        ya = _attn_lat(q, k, v, kc_b, vc_b, bias, l, dec_batch, dec_seq)
        ya = _attn_ctx(q, k, v, ya, n_lat, batch, seq)
        x, u2 = _merge(x, u, h1, ya, s, w3_b, w_in_b, w_o_b, mods[l], norm2_g[l][None], l, mod_row)
        if l + 1 < depth:
            x, u = _ffn(x, u2, wg_b, wu_b, wd_b, mods[l], norm1_g[l + 1][None], mods[l + 1], l, mod_row, False)
        else:
            ffn_args = (x, u2, wg_b, wu_b, wd_b, mods[l], final_g[None], mods[l], l, mod_row, True)
            y_sample = _ffn(*ffn_args, row_off=0, rows=n_lat)
            y_prompt = _ffn(*ffn_args, row_off=n_lat, rows=n_ctx)

    return (y_prompt.reshape(batch, seq, d), y_sample.reshape(dec_batch, dec_seq, d), new_kv[0], new_kv[1])
```

```python
import functools

import numpy as np
import jax
import jax.numpy as jnp
from jax import lax
from jax.experimental import pallas as pl
from jax.experimental.pallas import tpu as pltpu

F32 = jnp.float32
BF16 = jnp.bfloat16

D_MODEL = 1024
HEADS = 16
HEAD_DIM = 64
GRID_W = 64
WIN_H = 8
WIN_W = 16
CONV_K = 31
SC_K = 3
EPS = 1e-6
LANES = 128
SUBLANES = 8
MXU_COLS = 256
HALO = 16
MASK_VALUE = -1e30

TM_MM = 1024
TM_MERGE = 512
TM_CONV = 256
TN_PROJ = 512
NA_QROWS = 8
NA_LOOKAHEAD = 8
CTX_LANES = 512
NA_BAND_LANES = 1024
VMEM_LIMIT = 48 * 1024 * 1024


def _dot(a, b):
    return jnp.dot(a, b, preferred_element_type=F32)


def _dot_nt(a, b):
    return lax.dot_general(a, b, (((1,), (1,)), ((), ())), preferred_element_type=F32)


def _silu(x):
    return x * jax.nn.sigmoid(x)


def _norm_mod(x, g, scale, shift):
    xn = x * lax.rsqrt(jnp.mean(x * x, axis=-1, keepdims=True) + EPS)
    return (xn * g) * (1.0 + scale) + shift


def _params(*sem):
    return pltpu.CompilerParams(dimension_semantics=sem, vmem_limit_bytes=VMEM_LIMIT)


def _adaln_kernel(cond_ref, w_ref, b_ref, o_ref):
    a = _silu(cond_ref[...]).astype(BF16)
    o_ref[...] = _dot(a, w_ref[...].astype(BF16)) + b_ref[...]


def _adaln(cond, w_ada, b_ada):
    depth, d, n = w_ada.shape
    rows = cond.shape[0]
    tn = 1536
    return pl.pallas_call(
        _adaln_kernel,
        grid=(depth, n // tn),
        in_specs=[pl.BlockSpec((rows, d), lambda l, j: (0, 0)),
                  pl.BlockSpec((None, d, tn), lambda l, j: (l, 0, j)),
                  pl.BlockSpec((None, 1, tn), lambda l, j: (l, 0, j))],
        out_specs=pl.BlockSpec((None, rows, tn), lambda l, j: (l, 0, j)),
        out_shape=jax.ShapeDtypeStruct((depth, rows, n), F32),
        compiler_params=_params("parallel", "parallel"),
        name="adaln",
    )(cond, w_ada, b_ada.reshape(depth, 1, n))


def _prenorm_kernel(x_ref, g_ref, mod_ref, u_ref):
    m = mod_ref[0]
    u_ref[...] = _norm_mod(x_ref[...], g_ref[...], m[1:2], m[0:1]).astype(BF16)


def _prenorm(x, g, mods, mod_row):
    n, d = x.shape
    return pl.pallas_call(
        _prenorm_kernel,
        grid=(n // TM_MM,),
        in_specs=[pl.BlockSpec((TM_MM, d), lambda i: (i, 0)),
                  pl.BlockSpec((1, d), lambda i: (0, 0)),
                  pl.BlockSpec((1, 6, d), lambda i: (mod_row(i, TM_MM), 0, 0))],
        out_specs=pl.BlockSpec((TM_MM, d), lambda i: (i, 0)),
        out_shape=jax.ShapeDtypeStruct((n, d), BF16),
        compiler_params=_params("parallel"),
        name="prenorm",
    )(x, g, mods)


def _proj_glu_kernel(u_ref, wa_ref, wg_ref, o_ref):
    u = u_ref[...]
    o_ref[...] = (_dot(u, wa_ref[...]) * jax.nn.sigmoid(_dot(u, wg_ref[...]))).astype(BF16)


def _proj_sc_kernel(u_ref, wb_ref, wc_ref, wx_ref, t_ref, bg_ref):
    u = u_ref[...]
    t_ref[...] = (_dot(u, wc_ref[...]) * _dot(u, wx_ref[...])).astype(BF16)
    bg_ref[...] = _dot(u, wb_ref[...]).astype(BF16)


def _proj_qkv_kernel(u_ref, wq_ref, wk_ref, wv_ref, q_ref, k_ref, v_ref):
    u = u_ref[...]
    q_ref[...] = (_dot(u, wq_ref[...]) * (HEAD_DIM ** -0.5)).astype(BF16)
    k_ref[...] = _dot(u, wk_ref[...]).astype(BF16)
    v_ref[...] = _dot(u, wv_ref[...]).astype(BF16)


def _w_spec(l, col_off):
    off = col_off // TN_PROJ
    return pl.BlockSpec((None, D_MODEL, TN_PROJ), lambda i, j: (l, 0, j + off))


def _proj_call(kernel, u, w_in, l, col_offs, n_out, name):
    n = u.shape[0]
    out = [jax.ShapeDtypeStruct((n, D_MODEL), BF16)] * n_out
    out_specs = [pl.BlockSpec((TM_MM, TN_PROJ), lambda i, j: (i, j))] * n_out
    res = pl.pallas_call(
        kernel,
        grid=(n // TM_MM, D_MODEL // TN_PROJ),
        in_specs=[pl.BlockSpec((TM_MM, D_MODEL), lambda i, j: (i, 0))]
                 + [_w_spec(l, c) for c in col_offs],
        out_specs=out_specs if n_out > 1 else out_specs[0],
        out_shape=out if n_out > 1 else out[0],
        compiler_params=_params("parallel", "arbitrary"),
        name=name,
    )(u, *([w_in] * len(col_offs)))
    return res


def _proj_kv_ctx_kernel(u_ref, wk_ref, wv_ref, *rest):
    k_ref, v_ref = rest[-2:]
    u = u_ref[...]
    rk = _dot(u, wk_ref[...])
    rv = _dot(u, wv_ref[...])
    for h in range(TN_PROJ // HEAD_DIM):
        sl = slice(h * HEAD_DIM, (h + 1) * HEAD_DIM)
        k_ref[h] = rk[:, sl]
        v_ref[h] = rv[:, sl]


def _proj_kv_ctx(u, w_in, l, n_lat, batch, seq, depth, prev):
    heads_per = TN_PROJ // HEAD_DIM
    roff = n_lat // seq
    koff, voff = 3 * D_MODEL // TN_PROJ, 4 * D_MODEL // TN_PROJ
    out_spec = pl.BlockSpec((None, None, heads_per, seq, HEAD_DIM), lambda b, j: (b, l, j, 0, 0))
    out_sds = jax.ShapeDtypeStruct((batch, depth, HEADS, seq, HEAD_DIM), F32)
    in_specs = [pl.BlockSpec((seq, D_MODEL), lambda b, j: (b + roff, 0)),
                pl.BlockSpec((None, D_MODEL, TN_PROJ), lambda b, j: (l, 0, j + koff)),
                pl.BlockSpec((None, D_MODEL, TN_PROJ), lambda b, j: (l, 0, j + voff))]
    args = [u, w_in, w_in]
    aliases = {}
    if prev is not None:
        in_specs += [pl.BlockSpec(memory_space=pl.ANY)] * 2
        args += list(prev)
        aliases = {3: 0, 4: 1}
    return pl.pallas_call(
        _proj_kv_ctx_kernel,
        grid=(batch, D_MODEL // TN_PROJ),
        in_specs=in_specs,
        out_specs=[out_spec, out_spec],
        out_shape=[out_sds, out_sds],
        input_output_aliases=aliases,
        compiler_params=_params("parallel", "arbitrary"),
        name="proj_kv_ctx",
    )(*args)


def _fill_window(win_ref, prev_ref, cur_ref, next_ref, first, last, tm):
    prev = prev_ref[...].astype(F32)
    nxt = next_ref[...].astype(F32)
    win_ref[0:HALO, :] = jnp.where(first, 0.0, prev)
    win_ref[HALO:HALO + tm, :] = cur_ref[...].astype(F32)
    win_ref[HALO + tm:HALO + tm + HALO, :] = jnp.where(last, 0.0, nxt)


def _depthwise(win_ref, sh_ref, w_ref, out_ref, taps, tm):
    base = HALO - (taps - 1) // 2
    rows = 64
    sh_rows = sh_ref.shape[1]
    shifts = sorted({(base + k) % SUBLANES for k in range(taps)} - {0})

    def body(cb, carry):
        col = pl.multiple_of(cb * LANES, LANES)
        for s in shifts:
            sh_ref[s] = win_ref[pl.ds(s, sh_rows), pl.ds(col, LANES)]
        for r0 in range(0, tm, rows):
            acc = None
            for k in range(taps):
                off = base + r0 + k
                s, aligned = off % SUBLANES, off - off % SUBLANES
                if s == 0:
                    xk = win_ref[pl.ds(aligned, rows), pl.ds(col, LANES)]
                else:
                    xk = sh_ref[s, pl.ds(aligned, rows), :]
                term = xk * w_ref[pl.ds(k, 1), pl.ds(col, LANES)]
                acc = term if acc is None else acc + term
            out_ref[pl.ds(r0, rows), pl.ds(col, LANES)] = acc
        return carry

    lax.fori_loop(0, D_MODEL // LANES, body, 0)


def _conv_kernel(hp_ref, hc_ref, hn_ref, tp_ref, tc_ref, tn_ref, bg_ref,
                 wdw_ref, bdw_ref, lng_ref, lnb_ref, wsc_ref,
                 h1_ref, s_ref, win_ref, sh_ref, acc_ref, *, lat_tiles, lat_tiles_per_seq):
    tm = TM_CONV
    i = pl.program_id(0)
    is_lat = i < lat_tiles
    pos = i % lat_tiles_per_seq
    first = jnp.logical_or(jnp.logical_not(is_lat), pos == 0)
    last = jnp.logical_or(jnp.logical_not(is_lat), pos == lat_tiles_per_seq - 1)

    _fill_window(win_ref, hp_ref, hc_ref, hn_ref, first, last, tm)
    _depthwise(win_ref, sh_ref, wdw_ref, acc_ref, CONV_K, tm)
    y = acc_ref[...] + bdw_ref[...]
    mu = jnp.mean(y, axis=-1, keepdims=True)
    dlt = y - mu
    var = jnp.mean(dlt * dlt, axis=-1, keepdims=True)
    h = (dlt * lax.rsqrt(var + EPS)) * lng_ref[...] + lnb_ref[...]
    h1_ref[...] = _silu(h).astype(BF16)

    _fill_window(win_ref, tp_ref, tc_ref, tn_ref, first, last, tm)
    _depthwise(win_ref, sh_ref, wsc_ref, acc_ref, SC_K, tm)
    s_ref[...] = (bg_ref[...].astype(F32) * acc_ref[...]).astype(BF16)


def _conv_mixers(h0, t, bg, wdw, bdw, lng, lnb, wsc, lat_tokens, lat_seq):
    n, d = h0.shape
    tm = TM_CONV
    per = tm // HALO
    nblk = n // HALO
    cur = pl.BlockSpec((tm, d), lambda i: (i, 0))
    prev = pl.BlockSpec((HALO, d), lambda i: (jnp.maximum(i * per - 1, 0), 0))
    nxt = pl.BlockSpec((HALO, d), lambda i: (jnp.minimum((i + 1) * per, nblk - 1), 0))
    vec = pl.BlockSpec((1, d), lambda i: (0, 0))
    kern = functools.partial(_conv_kernel, lat_tiles=lat_tokens // tm, lat_tiles_per_seq=lat_seq // tm)
    return pl.pallas_call(
        kern,
        grid=(n // tm,),
        in_specs=[prev, cur, nxt, prev, cur, nxt, cur,
                  pl.BlockSpec((CONV_K, d), lambda i: (0, 0)), vec, vec, vec,
                  pl.BlockSpec((SC_K, d), lambda i: (0, 0))],
        out_specs=[cur, cur],
        out_shape=[jax.ShapeDtypeStruct((n, d), BF16)] * 2,
        scratch_shapes=[pltpu.VMEM((tm + 2 * HALO, d), F32),
                        pltpu.VMEM((SUBLANES, tm + 2 * HALO - SUBLANES, LANES), F32),
                        pltpu.VMEM((tm, d), F32)],
        compiler_params=_params("parallel"),
        name="conv_mixers",
    )(h0, h0, h0, t, t, t, bg, wdw, bdw, lng, lnb, wsc)


def _attn_ctx_kernel(q_ref, k_ref, v_ref, ya_hbm_ref, o_ref):
    del ya_hbm_ref
    for p in range(CTX_LANES // LANES):
        outs = []
        for h in range(LANES // HEAD_DIM):
            sl = slice(p * LANES + h * HEAD_DIM, p * LANES + (h + 1) * HEAD_DIM)
            s = _dot_nt(q_ref[:, sl], k_ref[:, sl])
            e = jnp.exp(s - jnp.max(_fold_lanes(jnp.maximum, s), axis=-1, keepdims=True))
            den = jnp.sum(_fold_lanes(jnp.add, e), axis=-1, keepdims=True)
            outs.append(_dot(e.astype(BF16), v_ref[:, sl]) / den)
        o_ref[:, p * LANES:(p + 1) * LANES] = jnp.concatenate(outs, axis=-1).astype(BF16)


def _attn_ctx(q, k, v, ya, lat_tokens, batch, seq):
    roff = lat_tokens // seq
    spec = pl.BlockSpec((seq, CTX_LANES), lambda b, hg: (b + roff, hg))
    return pl.pallas_call(
        _attn_ctx_kernel,
        grid=(batch, D_MODEL // CTX_LANES),
        in_specs=[spec, spec, spec, pl.BlockSpec(memory_space=pl.ANY)],
        out_specs=spec,
        out_shape=jax.ShapeDtypeStruct(ya.shape, BF16),
        input_output_aliases={3: 0},
        compiler_params=_params("parallel", "parallel"),
        name="attn_ctx",
    )(q, k, v, ya)


def _fold_lanes(op, *arrays):
    tiles = [a[:, i * LANES:(i + 1) * LANES] for a in arrays for i in range(a.shape[1] // LANES)]
    return functools.reduce(op, tiles)


def _attn_lat_kernel(q_ref, k_ref, v_ref, kc_ref, vc_ref, bias_ref, o_ref, ko_ref, vo_ref, *, grid_rows):
    rb = pl.program_id(2)
    nloc = WIN_H * GRID_W

    @pl.when(rb == 0)
    def _():
        ko_ref[...] = k_ref[:, HEAD_DIM:]
        vo_ref[...] = v_ref[:, HEAD_DIM:]

    starts, band_sel, band_off = [], [], []
    for j in range(NA_QROWS):
        r = rb * NA_QROWS + j
        rs = jnp.clip(r - WIN_H // 2, 0, grid_rows - WIN_H)
        starts.append(pl.multiple_of(rs * GRID_W, GRID_W))
        lane0 = (WIN_H - 1 - (r - rs)) * GRID_W
        band_sel.append((lane0 // GRID_W) % (LANES // GRID_W))
        band_off.append(pl.multiple_of((lane0 // LANES) * LANES, LANES))

    n_heads = LANES // HEAD_DIM
    qs = [q_ref[:, h * HEAD_DIM:(h + 1) * HEAD_DIM] for h in range(n_heads)]
    s_ctx = [_dot_nt(qs[h], kc_ref[h]) for h in range(n_heads)]

    chains = [(h, j) for h in range(n_heads) for j in range(NA_QROWS)]
    s_loc, o_rows, p_rows, dens = {}, {}, {}, {}
    for t in range(len(chains) + NA_LOOKAHEAD):
        if t < len(chains):
            h, j = chains[t]
            kw = k_ref[pl.ds(starts[j], nloc), :HEAD_DIM] if h == 0 else ko_ref[pl.ds(starts[j], nloc), :]
            bias = bias_ref[h, band_sel[j], :, pl.ds(band_off[j], nloc)]
            s_loc[t] = _dot_nt(qs[h][j * GRID_W:(j + 1) * GRID_W], kw) + bias
        if t >= NA_LOOKAHEAD:
            c = t - NA_LOOKAHEAD
            h, j = chains[c]
            vw = v_ref[pl.ds(starts[j], nloc), :HEAD_DIM] if h == 0 else vo_ref[pl.ds(starts[j], nloc), :]
            s_l = s_loc.pop(c)
            s_c = s_ctx[h][j * GRID_W:(j + 1) * GRID_W]
            m = jnp.max(_fold_lanes(jnp.maximum, s_l, s_c), axis=-1, keepdims=True)
            e_loc = jnp.exp(s_l - m)
            e_ctx = jnp.exp(s_c - m)
            dens[c] = jnp.sum(_fold_lanes(jnp.add, e_loc, e_ctx), axis=-1, keepdims=True)
            o_rows[c] = _dot(e_loc.astype(BF16), vw)
            p_rows[c] = e_ctx.astype(BF16)

    outs = []
    for h in range(n_heads):
        ids = [c for c, (hh, _) in enumerate(chains) if hh == h]
        o_ctx = _dot(jnp.concatenate([p_rows[c] for c in ids], axis=0), vc_ref[h])
        o = jnp.concatenate([o_rows[c] for c in ids], axis=0) + o_ctx
        outs.append(o / jnp.concatenate([dens[c] for c in ids], axis=0))
    o_ref[...] = jnp.concatenate(outs, axis=-1).astype(BF16)


def _attn_lat(q, k, v, kc, vc, bias, l, dec_batch, dec_seq):
    grid_rows = dec_seq // GRID_W
    nrb = grid_rows // NA_QROWS
    tq = NA_QROWS * GRID_W
    hp_n = LANES // HEAD_DIM
    past = kc.shape[3]
    seq_spec = pl.BlockSpec((dec_seq, LANES), lambda hp, b, rb: (b, hp))
    ctx_spec = pl.BlockSpec((None, None, hp_n, past, HEAD_DIM), lambda hp, b, rb: (b, l, hp, 0, 0))
    q_spec = pl.BlockSpec((tq, LANES), lambda hp, b, rb: (b * nrb + rb, hp))
    return pl.pallas_call(
        functools.partial(_attn_lat_kernel, grid_rows=grid_rows),
        grid=(D_MODEL // LANES, dec_batch, nrb),
        in_specs=[q_spec, seq_spec, seq_spec, ctx_spec, ctx_spec,
                  pl.BlockSpec((None, hp_n, LANES // GRID_W, GRID_W, NA_BAND_LANES),
                               lambda hp, b, rb: (l, hp, 0, 0, 0))],
        out_specs=q_spec,
        out_shape=jax.ShapeDtypeStruct(q.shape, BF16),
        scratch_shapes=[pltpu.VMEM((dec_seq, HEAD_DIM), BF16)] * 2,
        compiler_params=_params("arbitrary", "arbitrary", "arbitrary"),
        name="attn_lat",
    )(q, k, v, kc, vc, bias)


def _na_bias_tables(rpb):
    c = np.arange(GRID_W)
    cs = np.clip(c - WIN_W // 2, 0, GRID_W - WIN_W)
    kc = np.arange(GRID_W)
    col_ok = (kc[None, :] >= cs[:, None]) & (kc[None, :] < cs[:, None] + WIN_W)
    dcol = kc[None, :] - c[:, None] + WIN_W - 1
    n_row, n_col = 2 * WIN_H - 1, 2 * WIN_W - 1
    onehot = (np.arange(n_col)[:, None] == dcol.reshape(1, -1)).astype(np.float32)
    toep = jnp.einsum('lhid,dn->lhin', rpb, onehot, precision=lax.Precision.HIGHEST)
    depth, heads = rpb.shape[0], rpb.shape[1]
    toep = jnp.where(col_ok[None, None, None], toep.reshape(depth, heads, n_row, GRID_W, GRID_W), MASK_VALUE)
    band = jnp.transpose(toep, (0, 1, 3, 2, 4)).reshape(depth, heads, GRID_W, n_row * GRID_W)
    band = jnp.pad(band, ((0, 0), (0, 0), (0, 0), (0, NA_BAND_LANES + GRID_W - n_row * GRID_W)))
    return jnp.stack([band[..., :NA_BAND_LANES], band[..., GRID_W:GRID_W + NA_BAND_LANES]], axis=2)


def _merge_kernel(x_ref, u_ref, h1_ref, ya_ref, s_ref, w3_ref, wgc_ref, wgn_ref, wgs_ref, wo_ref, mod_ref, g2_ref,
                  x1_ref, u2_ref):
    u = u_ref[...]
    branches = ((h1_ref, wgc_ref), (ya_ref, wgn_ref), (s_ref, wgs_ref))
    y = None
    for c in range(D_MODEL // MXU_COLS):
        cs = slice(c * MXU_COLS, (c + 1) * MXU_COLS)
        mixed = None
        for b, (inp_ref, wg_ref) in enumerate(branches):
            term = jax.nn.sigmoid(_dot(u, wg_ref[:, cs])) * _dot(inp_ref[...], w3_ref[b, :, cs])
            mixed = term if mixed is None else mixed + term
        part = _dot(mixed.astype(BF16), wo_ref[cs, :])
        y = part if y is None else y + part
    m = mod_ref[0]
    x1 = x_ref[...] + m[2:3] * y
    x1_ref[...] = x1
    u2_ref[...] = _norm_mod(x1, g2_ref[...], m[4:5], m[3:4]).astype(BF16)


def _resident(block_shape, index_map):
    return pl.BlockSpec(block_shape, index_map, pipeline_mode=pl.Buffered(1))


def _merge(x, u, h1, ya, s, w3, w_in, w_o, mods, g2, l, mod_row):
    n, d = x.shape
    tm = TM_MERGE
    gate_off = (w_in.shape[2] - 3 * d) // d
    tile = pl.BlockSpec((tm, d), lambda i: (i, 0))
    gate_specs = [_resident((None, d, d), functools.partial(lambda b, i: (l, 0, gate_off + b), b)) for b in range(3)]
    return pl.pallas_call(
        _merge_kernel,
        grid=(n // tm,),
        in_specs=[tile, tile, tile, tile, tile,
                  _resident((None, 3, d, d), lambda i: (l, 0, 0, 0))] + gate_specs
                 + [_resident((None, d, d), lambda i: (l, 0, 0)),
                    pl.BlockSpec((1, 6, d), lambda i: (mod_row(i, tm), 0, 0)),
                    pl.BlockSpec((1, d), lambda i: (0, 0))],
        out_specs=[tile, tile],
        out_shape=[jax.ShapeDtypeStruct((n, d), F32), jax.ShapeDtypeStruct((n, d), BF16)],
        compiler_params=_params("parallel"),
        name="merge",
    )(x, u, h1, ya, s, w3, w_in, w_in, w_in, w_o, mods, g2)


def _ffn_kernel(x_ref, u_ref, wg_ref, wu_ref, wd_ref, mod_ref, gn_ref, modn_ref, *rest, final):
    u = u_ref[...]
    f = None
    for c in range(wg_ref.shape[1] // MXU_COLS):
        cs = slice(c * MXU_COLS, (c + 1) * MXU_COLS)
        hidden = _silu(_dot(u, wg_ref[:, cs])) * _dot(u, wu_ref[:, cs])
        part = _dot(hidden.astype(BF16), wd_ref[cs, :])
        f = part if f is None else f + part
    x2 = x_ref[...] + mod_ref[0][5:6] * f
    if final:
        (y_ref,) = rest
        xn = x2 * lax.rsqrt(jnp.mean(x2 * x2, axis=-1, keepdims=True) + EPS)
        y_ref[...] = xn * gn_ref[...]
    else:
        x2_ref, un_ref = rest
        mn = modn_ref[0]
        x2_ref[...] = x2
        un_ref[...] = _norm_mod(x2, gn_ref[...], mn[1:2], mn[0:1]).astype(BF16)


def _ffn(x, u, wg, wu, wd, mods, g_next, mods_next, l, mod_row, final, row_off=0, rows=None):
    d = x.shape[1]
    n = x.shape[0] if rows is None else rows
    tm = TM_MERGE
    dff = wg.shape[2]
    assert dff % MXU_COLS == 0
    toff = row_off // tm
    tile = pl.BlockSpec((tm, d), lambda i: (i + toff, 0))
    out_tile = pl.BlockSpec((tm, d), lambda i: (i, 0))
    modspec = pl.BlockSpec((1, 6, d), lambda i: (mod_row(i + toff, tm), 0, 0))
    if final:
        out_shape = jax.ShapeDtypeStruct((n, d), F32)
        out_specs = out_tile
    else:
        out_shape = [jax.ShapeDtypeStruct((n, d), F32), jax.ShapeDtypeStruct((n, d), BF16)]
        out_specs = [out_tile, out_tile]
    return pl.pallas_call(
        functools.partial(_ffn_kernel, final=final),
        grid=(n // tm,),
        in_specs=[tile, tile,
                  _resident((None, d, dff), lambda i: (l, 0, 0)),
                  _resident((None, d, dff), lambda i: (l, 0, 0)),
                  _resident((None, dff, d), lambda i: (l, 0, 0)),
                  modspec,
                  pl.BlockSpec((1, d), lambda i: (0, 0)),
                  modspec],
        out_specs=out_specs,
        out_shape=out_shape,
        compiler_params=_params("parallel"),
        name="ffn_final" if final else "ffn",
    )(x, u, wg, wu, wd, mods, g_next, mods_next)


def kernel(x_prompt, x_sample, cache_k, cache_v, c, c_ctx, w_ada, b_ada, norm1_g, w_in, conv_dw_w, conv_dw_b, conv_ln_g, conv_ln_b, conv_pw_w, sc_dw_w, sc_out_w, na_rpb, na_out_w, w_o, norm2_g, ffn_w_gate, ffn_w_up, ffn_w_down, final_g):
    batch, seq, d = x_prompt.shape
    dec_batch, dec_seq, _ = x_sample.shape
    depth = w_ada.shape[0]
    n_lat = dec_batch * dec_seq
    n_ctx = batch * seq
    grid_rows = dec_seq // GRID_W
    assert d == D_MODEL and w_in.shape[2] == 11 * d
    assert seq == TM_CONV and dec_seq % TM_MM == 0 and n_ctx % TM_MM == 0
    assert grid_rows % NA_QROWS == 0 and grid_rows >= WIN_H and n_lat % TM_MERGE == 0

    def mod_row(i, tm):
        return jnp.minimum((i * tm) // dec_seq, dec_batch)

    n_cond = -(-(dec_batch + 1) // 8) * 8
    cond = jnp.zeros((n_cond, d), F32).at[:dec_batch].set(c).at[dec_batch].set(c_ctx)
    mods = _adaln(cond, w_ada, b_ada).reshape(depth, n_cond, 6, d)

    w_in_b = w_in.astype(BF16)
    w3_b = jnp.stack([conv_pw_w, na_out_w, sc_out_w], axis=1).astype(BF16)
    w_o_b = w_o.astype(BF16)
    wg_b, wu_b, wd_b = ffn_w_gate.astype(BF16), ffn_w_up.astype(BF16), ffn_w_down.astype(BF16)
    kc_b, vc_b = cache_k.astype(BF16), cache_v.astype(BF16)
    bias = _na_bias_tables(na_rpb)

    x = jnp.concatenate([x_sample.reshape(n_lat, d), x_prompt.reshape(n_ctx, d)], axis=0)
    u = _prenorm(x, norm1_g[0][None], mods[0], mod_row)
    new_kv = None
    for l in range(depth):
        h0 = _proj_call(_proj_glu_kernel, u, w_in_b, l, (0, d), 1, "proj_glu")
        q, k, v = _proj_call(_proj_qkv_kernel, u, w_in_b, l, (2 * d, 3 * d, 4 * d), 3, "proj_qkv")
        new_kv = _proj_kv_ctx(u, w_in_b, l, n_lat, batch, seq, depth, new_kv)
        t, bg = _proj_call(_proj_sc_kernel, u, w_in_b, l, (5 * d, 6 * d, 7 * d), 2, "proj_sc")
        h1, s = _conv_mixers(h0, t, bg, conv_dw_w[l], conv_dw_b[l][None], conv_ln_g[l][None],
                             conv_ln_b[l][None], sc_dw_w[l], n_lat, dec_seq)
        ya = _attn_lat(q, k, v, kc_b, vc_b, bias, l, dec_batch, dec_seq)
        ya = _attn_ctx(q, k, v, ya, n_lat, batch, seq)
        x, u2 = _merge(x, u, h1, ya, s, w3_b, w_in_b, w_o_b, mods[l], norm2_g[l][None], l, mod_row)
        if l + 1 < depth:
            x, u = _ffn(x, u2, wg_b, wu_b, wd_b, mods[l], norm1_g[l + 1][None], mods[l + 1], l, mod_row, False)
        else:
            ffn_args = (x, u2, wg_b, wu_b, wd_b, mods[l], final_g[None], mods[l], l, mod_row, True)
            y_sample = _ffn(*ffn_args, row_off=0, rows=n_lat)
            y_prompt = _ffn(*ffn_args, row_off=n_lat, rows=n_ctx)

    return (y_prompt.reshape(batch, seq, d), y_sample.reshape(dec_batch, dec_seq, d), new_kv[0], new_kv[1])
```

```python
import functools

import numpy as np
import jax
import jax.numpy as jnp
from jax import lax
from jax.experimental import pallas as pl
from jax.experimental.pallas import tpu as pltpu

F32 = jnp.float32
BF16 = jnp.bfloat16

D_MODEL = 1024
HEADS = 16
HEAD_DIM = 64
GRID_W = 64
WIN_H = 8
WIN_W = 16
CONV_K = 31
SC_K = 3
EPS = 1e-6
LANES = 128
SUBLANES = 8
MXU_COLS = 256
HALO = 16
MASK_VALUE = -1e30

TM_MM = 1024
TM_MERGE = 512
TM_CONV = 256
TN_PROJ = 512
NA_QROWS = 8
CTX_LANES = 512
NA_BAND_LANES = 1024
VMEM_LIMIT = 48 * 1024 * 1024


def _dot(a, b):
    return jnp.dot(a, b, preferred_element_type=F32)


def _dot_nt(a, b):
    return lax.dot_general(a, b, (((1,), (1,)), ((), ())), preferred_element_type=F32)


def _silu(x):
    return x * jax.nn.sigmoid(x)


def _norm_mod(x, g, scale, shift):
    xn = x * lax.rsqrt(jnp.mean(x * x, axis=-1, keepdims=True) + EPS)
    return (xn * g) * (1.0 + scale) + shift


def _params(*sem):
    return pltpu.CompilerParams(dimension_semantics=sem, vmem_limit_bytes=VMEM_LIMIT)


def _resident(block_shape, index_map):
    return pl.BlockSpec(block_shape, index_map, pipeline_mode=pl.Buffered(1))


def _adaln_kernel(cond_ref, w_ref, b_ref, o_ref):
    a = _silu(cond_ref[...]).astype(BF16)
    o_ref[...] = _dot(a, w_ref[...].astype(BF16)) + b_ref[...]


def _adaln(cond, w_ada, b_ada):
    depth, d, n = w_ada.shape
    rows = cond.shape[0]
    tn = 1536
    return pl.pallas_call(
        _adaln_kernel,
        grid=(depth, n // tn),
        in_specs=[pl.BlockSpec((rows, d), lambda l, j: (0, 0)),
                  pl.BlockSpec((None, d, tn), lambda l, j: (l, 0, j)),
                  pl.BlockSpec((None, 1, tn), lambda l, j: (l, 0, j))],
        out_specs=pl.BlockSpec((None, rows, tn), lambda l, j: (l, 0, j)),
        out_shape=jax.ShapeDtypeStruct((depth, rows, n), F32),
        compiler_params=_params("parallel", "parallel"),
        name="adaln",
    )(cond, w_ada, b_ada.reshape(depth, 1, n))


def _split_specs(tm, lat_tiles):
    lat = pl.BlockSpec((tm, D_MODEL), lambda i: (jnp.minimum(i, lat_tiles - 1), 0))
    ctx = pl.BlockSpec((tm, D_MODEL), lambda i: (jnp.maximum(i - lat_tiles, 0), 0))
    return [lat, ctx]


def _pick(lat_tiles, lat_ref, ctx_ref):
    return jnp.where(pl.program_id(0) < lat_tiles, lat_ref[...], ctx_ref[...])


def _prenorm_kernel(xl_ref, xc_ref, g_ref, mod_ref, u_ref, *, lat_tiles):
    m = mod_ref[0]
    u_ref[...] = _norm_mod(_pick(lat_tiles, xl_ref, xc_ref), g_ref[...], m[1:2], m[0:1]).astype(BF16)


def _prenorm(x_lat, x_ctx, g, mods, mod_row):
    d = x_lat.shape[1]
    n = x_lat.shape[0] + x_ctx.shape[0]
    lat_tiles = x_lat.shape[0] // TM_MM
    return pl.pallas_call(
        functools.partial(_prenorm_kernel, lat_tiles=lat_tiles),
        grid=(n // TM_MM,),
        in_specs=_split_specs(TM_MM, lat_tiles)
                 + [pl.BlockSpec((1, d), lambda i: (0, 0)),
                    pl.BlockSpec((1, 6, d), lambda i: (mod_row(i, TM_MM), 0, 0))],
        out_specs=pl.BlockSpec((TM_MM, d), lambda i: (i, 0)),
        out_shape=jax.ShapeDtypeStruct((n, d), BF16),
        compiler_params=_params("parallel"),
        name="prenorm",
    )(x_lat, x_ctx, g, mods)


def _proj_kernel(u_ref, w_ref, h0_ref, q_ref, k_ref, v_ref, t_ref, bg_ref):
    u = u_ref[...]

    def seg(n, cs):
        return _dot(u, w_ref[:, n * D_MODEL + cs.start:n * D_MODEL + cs.stop])

    for c in range(D_MODEL // MXU_COLS):
        cs = slice(c * MXU_COLS, (c + 1) * MXU_COLS)
        h0_ref[:, cs] = (seg(0, cs) * jax.nn.sigmoid(seg(1, cs))).astype(BF16)
        q_ref[:, cs] = (seg(2, cs) * (HEAD_DIM ** -0.5)).astype(BF16)
        k_ref[:, cs] = seg(3, cs).astype(BF16)
        v_ref[:, cs] = seg(4, cs).astype(BF16)
        bg_ref[:, cs] = seg(5, cs).astype(BF16)
        t_ref[:, cs] = (seg(6, cs) * seg(7, cs)).astype(BF16)


def _proj(u, w_in, l):
    n, d = u.shape
    tm = TM_MERGE
    tile = pl.BlockSpec((tm, d), lambda i: (i, 0))
    return pl.pallas_call(
        _proj_kernel,
        grid=(n // tm,),
        in_specs=[tile, _resident((None, d, 8 * d), lambda i: (l, 0, 0))],
        out_specs=[tile] * 6,
        out_shape=[jax.ShapeDtypeStruct((n, d), BF16)] * 6,
        compiler_params=_params("parallel"),
        name="proj",
    )(u, w_in)


def _proj_kv_ctx_kernel(u_ref, wk_ref, wv_ref, k_ref, v_ref):
    u = u_ref[...]
    rk = _dot(u, wk_ref[...])
    rv = _dot(u, wv_ref[...])
    for h in range(TN_PROJ // HEAD_DIM):
        sl = slice(h * HEAD_DIM, (h + 1) * HEAD_DIM)
        k_ref[h] = rk[:, sl]
        v_ref[h] = rv[:, sl]


def _proj_kv_ctx(u_ctx, w_in, batch, seq):
    depth = u_ctx.shape[0]
    heads_per = TN_PROJ // HEAD_DIM
    koff, voff = 3 * D_MODEL // TN_PROJ, 4 * D_MODEL // TN_PROJ
    out_spec = pl.BlockSpec((None, None, heads_per, seq, HEAD_DIM), lambda l, j, b: (b, l, j, 0, 0))
    out_sds = jax.ShapeDtypeStruct((batch, depth, HEADS, seq, HEAD_DIM), F32)
    return pl.pallas_call(
        _proj_kv_ctx_kernel,
        grid=(depth, D_MODEL // TN_PROJ, batch),
        in_specs=[pl.BlockSpec((None, seq, D_MODEL), lambda l, j, b: (l, b, 0)),
                  pl.BlockSpec((None, D_MODEL, TN_PROJ), lambda l, j, b: (l, 0, j + koff)),
                  pl.BlockSpec((None, D_MODEL, TN_PROJ), lambda l, j, b: (l, 0, j + voff))],
        out_specs=[out_spec, out_spec],
        out_shape=[out_sds, out_sds],
        compiler_params=_params("parallel", "parallel", "arbitrary"),
        name="proj_kv_ctx",
    )(u_ctx, w_in, w_in)


def _fill_window(win_ref, prev_ref, cur_ref, next_ref, first, last, tm):
    prev = prev_ref[...].astype(F32)
    nxt = next_ref[...].astype(F32)
    win_ref[0:HALO, :] = jnp.where(first, 0.0, prev)
    win_ref[HALO:HALO + tm, :] = cur_ref[...].astype(F32)
    win_ref[HALO + tm:HALO + tm + HALO, :] = jnp.where(last, 0.0, nxt)


def _depthwise(win_ref, sh_ref, w_ref, out_ref, taps, tm):
    base = HALO - (taps - 1) // 2
    rows = 64
    sh_rows = sh_ref.shape[1]
    shifts = sorted({(base + k) % SUBLANES for k in range(taps)} - {0})

    def body(cb, carry):
        col = pl.multiple_of(cb * LANES, LANES)
        for s in shifts:
            sh_ref[s] = win_ref[pl.ds(s, sh_rows), pl.ds(col, LANES)]
        for r0 in range(0, tm, rows):
            acc = None
            for k in range(taps):
                off = base + r0 + k
                s, aligned = off % SUBLANES, off - off % SUBLANES
                if s == 0:
                    xk = win_ref[pl.ds(aligned, rows), pl.ds(col, LANES)]
                else:
                    xk = sh_ref[s, pl.ds(aligned, rows), :]
                term = xk * w_ref[pl.ds(k, 1), pl.ds(col, LANES)]
                acc = term if acc is None else acc + term
            out_ref[pl.ds(r0, rows), pl.ds(col, LANES)] = acc
        return carry

    lax.fori_loop(0, D_MODEL // LANES, body, 0)


def _conv_kernel(hp_ref, hc_ref, hn_ref, tp_ref, tc_ref, tn_ref, bg_ref,
                 wdw_ref, bdw_ref, lng_ref, lnb_ref, wsc_ref,
                 h1_ref, s_ref, win_ref, sh_ref, acc_ref, *, lat_tiles, lat_tiles_per_seq):
    tm = TM_CONV
    i = pl.program_id(0)
    is_lat = i < lat_tiles
    pos = i % lat_tiles_per_seq
    first = jnp.logical_or(jnp.logical_not(is_lat), pos == 0)
    last = jnp.logical_or(jnp.logical_not(is_lat), pos == lat_tiles_per_seq - 1)

    _fill_window(win_ref, hp_ref, hc_ref, hn_ref, first, last, tm)
    _depthwise(win_ref, sh_ref, wdw_ref, acc_ref, CONV_K, tm)
    y = acc_ref[...] + bdw_ref[...]
    mu = jnp.mean(y, axis=-1, keepdims=True)
    dlt = y - mu
    var = jnp.mean(dlt * dlt, axis=-1, keepdims=True)
    h = (dlt * lax.rsqrt(var + EPS)) * lng_ref[...] + lnb_ref[...]
    h1_ref[...] = _silu(h).astype(BF16)

    _fill_window(win_ref, tp_ref, tc_ref, tn_ref, first, last, tm)
    _depthwise(win_ref, sh_ref, wsc_ref, acc_ref, SC_K, tm)
    s_ref[...] = (bg_ref[...].astype(F32) * acc_ref[...]).astype(BF16)


def _conv_mixers(h0, t, bg, wdw, bdw, lng, lnb, wsc, lat_tokens, lat_seq):
    n, d = h0.shape
    tm = TM_CONV
    per = tm // HALO
    nblk = n // HALO
    cur = pl.BlockSpec((tm, d), lambda i: (i, 0))
    prev = pl.BlockSpec((HALO, d), lambda i: (jnp.maximum(i * per - 1, 0), 0))
    nxt = pl.BlockSpec((HALO, d), lambda i: (jnp.minimum((i + 1) * per, nblk - 1), 0))
    vec = pl.BlockSpec((1, d), lambda i: (0, 0))
    kern = functools.partial(_conv_kernel, lat_tiles=lat_tokens // tm, lat_tiles_per_seq=lat_seq // tm)
    return pl.pallas_call(
        kern,
        grid=(n // tm,),
        in_specs=[prev, cur, nxt, prev, cur, nxt, cur,
                  pl.BlockSpec((CONV_K, d), lambda i: (0, 0)), vec, vec, vec,
                  pl.BlockSpec((SC_K, d), lambda i: (0, 0))],
        out_specs=[cur, cur],
        out_shape=[jax.ShapeDtypeStruct((n, d), BF16)] * 2,
        scratch_shapes=[pltpu.VMEM((tm + 2 * HALO, d), F32),
                        pltpu.VMEM((SUBLANES, tm + 2 * HALO - SUBLANES, LANES), F32),
                        pltpu.VMEM((tm, d), F32)],
        compiler_params=_params("parallel"),
        name="conv_mixers",
    )(h0, h0, h0, t, t, t, bg, wdw, bdw, lng, lnb, wsc)


def _fold_lanes(op, *arrays):
    tiles = [a[:, i * LANES:(i + 1) * LANES] for a in arrays for i in range(a.shape[1] // LANES)]
    return functools.reduce(op, tiles)


def _attn_ctx_kernel(q_ref, k_ref, v_ref, o_ref):
    for p in range(CTX_LANES // LANES):
        outs = []
        for h in range(LANES // HEAD_DIM):
            sl = slice(p * LANES + h * HEAD_DIM, p * LANES + (h + 1) * HEAD_DIM)
            s = _dot_nt(q_ref[:, sl], k_ref[:, sl])
            e = jnp.exp(s - jnp.max(_fold_lanes(jnp.maximum, s), axis=-1, keepdims=True))
            den = jnp.sum(_fold_lanes(jnp.add, e), axis=-1, keepdims=True)
            outs.append(_dot(e.astype(BF16), v_ref[:, sl]) / den)
        o_ref[:, p * LANES:(p + 1) * LANES] = jnp.concatenate(outs, axis=-1).astype(BF16)


def _attn_ctx(q, k, v, lat_tokens, batch, seq):
    roff = lat_tokens // seq
    spec = pl.BlockSpec((seq, CTX_LANES), lambda b, hg: (b + roff, hg))
    return pl.pallas_call(
        _attn_ctx_kernel,
        grid=(batch, D_MODEL // CTX_LANES),
        in_specs=[spec, spec, spec],
        out_specs=pl.BlockSpec((seq, CTX_LANES), lambda b, hg: (b, hg)),
        out_shape=jax.ShapeDtypeStruct((batch * seq, D_MODEL), BF16),
        compiler_params=_params("parallel", "parallel"),
        name="attn_ctx",
    )(q, k, v)


def _attn_lat_kernel(q_ref, k_ref, v_ref, kc_ref, vc_ref, bias_ref, o_ref, ko_ref, vo_ref, *, grid_rows):
    rb = pl.program_id(2)
    nloc = WIN_H * GRID_W

    @pl.when(rb == 0)
    def _():
        ko_ref[...] = k_ref[:, HEAD_DIM:]
        vo_ref[...] = v_ref[:, HEAD_DIM:]

    starts, band_sel, band_off = [], [], []
    for j in range(NA_QROWS):
        r = rb * NA_QROWS + j
        rs = jnp.clip(r - WIN_H // 2, 0, grid_rows - WIN_H)
        starts.append(pl.multiple_of(rs * GRID_W, GRID_W))
        lane0 = (WIN_H - 1 - (r - rs)) * GRID_W
        band_sel.append((lane0 // GRID_W) % (LANES // GRID_W))
        band_off.append(pl.multiple_of((lane0 // LANES) * LANES, LANES))

    heads = range(LANES // HEAD_DIM)
    scores = []
    for h in heads:
        q = q_ref[:, h * HEAD_DIM:(h + 1) * HEAD_DIM]
        s_ctx = _dot_nt(q, kc_ref[h].astype(BF16))
        s_rows = []
        for j in range(NA_QROWS):
            kw = k_ref[pl.ds(starts[j], nloc), :HEAD_DIM] if h == 0 else ko_ref[pl.ds(starts[j], nloc), :]
            bias = bias_ref[h, band_sel[j], :, pl.ds(band_off[j], nloc)]
            s_rows.append(_dot_nt(q[j * GRID_W:(j + 1) * GRID_W], kw) + bias)
        scores.append((jnp.concatenate(s_rows, axis=0), s_ctx))
    probs = []
    for s_loc, s_ctx in scores:
        m = jnp.max(_fold_lanes(jnp.maximum, s_loc, s_ctx), axis=-1, keepdims=True)
        e_loc = jnp.exp(s_loc - m)
        e_ctx = jnp.exp(s_ctx - m)
        den = jnp.sum(_fold_lanes(jnp.add, e_loc, e_ctx), axis=-1, keepdims=True)
        probs.append((e_loc.astype(BF16), e_ctx.astype(BF16), den))
    outs = []
    for h in heads:
        p_loc, p_ctx, den = probs[h]
        o_rows = []
        for j in range(NA_QROWS):
            vw = v_ref[pl.ds(starts[j], nloc), :HEAD_DIM] if h == 0 else vo_ref[pl.ds(starts[j], nloc), :]
            o_rows.append(_dot(p_loc[j * GRID_W:(j + 1) * GRID_W], vw))
        o = jnp.concatenate(o_rows, axis=0) + _dot(p_ctx, vc_ref[h].astype(BF16))
        outs.append(o / den)
    o_ref[...] = jnp.concatenate(outs, axis=-1).astype(BF16)


def _attn_lat(q, k, v, kc, vc, bias, l, dec_batch, dec_seq):
    grid_rows = dec_seq // GRID_W
    nrb = grid_rows // NA_QROWS
    tq = NA_QROWS * GRID_W
    hp_n = LANES // HEAD_DIM
    past = kc.shape[3]
    seq_spec = pl.BlockSpec((dec_seq, LANES), lambda hp, b, rb: (b, hp))
    ctx_spec = pl.BlockSpec((None, None, hp_n, past, HEAD_DIM), lambda hp, b, rb: (b, l, hp, 0, 0))
    q_spec = pl.BlockSpec((tq, LANES), lambda hp, b, rb: (b * nrb + rb, hp))
    return pl.pallas_call(
        functools.partial(_attn_lat_kernel, grid_rows=grid_rows),
        grid=(D_MODEL // LANES, dec_batch, nrb),
        in_specs=[q_spec, seq_spec, seq_spec, ctx_spec, ctx_spec,
                  pl.BlockSpec((None, hp_n, LANES // GRID_W, GRID_W, NA_BAND_LANES),
                               lambda hp, b, rb: (l, hp, 0, 0, 0))],
        out_specs=q_spec,
        out_shape=jax.ShapeDtypeStruct((dec_batch * dec_seq, D_MODEL), BF16),
        scratch_shapes=[pltpu.VMEM((dec_seq, HEAD_DIM), BF16)] * 2,
        compiler_params=_params("arbitrary", "arbitrary", "arbitrary"),
        name="attn_lat",
    )(q, k, v, kc, vc, bias)


def _na_bias_tables(rpb):
    c = np.arange(GRID_W)
    cs = np.clip(c - WIN_W // 2, 0, GRID_W - WIN_W)
    kc = np.arange(GRID_W)
    col_ok = (kc[None, :] >= cs[:, None]) & (kc[None, :] < cs[:, None] + WIN_W)
    dcol = kc[None, :] - c[:, None] + WIN_W - 1
    n_row, n_col = 2 * WIN_H - 1, 2 * WIN_W - 1
    onehot = (np.arange(n_col)[:, None] == dcol.reshape(1, -1)).astype(np.float32)
    toep = jnp.einsum('lhid,dn->lhin', rpb, onehot, precision=lax.Precision.HIGHEST)
    depth, heads = rpb.shape[0], rpb.shape[1]
    toep = jnp.where(col_ok[None, None, None], toep.reshape(depth, heads, n_row, GRID_W, GRID_W), MASK_VALUE)
    band = jnp.transpose(toep, (0, 1, 3, 2, 4)).reshape(depth, heads, GRID_W, n_row * GRID_W)
    band = jnp.pad(band, ((0, 0), (0, 0), (0, 0), (0, NA_BAND_LANES + GRID_W - n_row * GRID_W)))
    return jnp.stack([band[..., :NA_BAND_LANES], band[..., GRID_W:GRID_W + NA_BAND_LANES]], axis=2)


def _merge_kernel(*refs, lat_tiles, split_x):
    x_refs, refs = refs[:1 + split_x], refs[1 + split_x:]
    (u_ref, h1_ref, yl_ref, yc_ref, s_ref, w3_ref, wgc_ref, wgn_ref, wgs_ref, wo_ref, mod_ref, g2_ref,
     x1_ref, u2_ref) = refs
    u = u_ref[...]
    branches = ((h1_ref[...], wgc_ref), (_pick(lat_tiles, yl_ref, yc_ref), wgn_ref), (s_ref[...], wgs_ref))
    y = None
    for c in range(D_MODEL // MXU_COLS):
        cs = slice(c * MXU_COLS, (c + 1) * MXU_COLS)
        mixed = None
        for b, (inp, wg_ref) in enumerate(branches):
            term = jax.nn.sigmoid(_dot(u, wg_ref[:, cs])) * _dot(inp, w3_ref[b, :, cs])
            mixed = term if mixed is None else mixed + term
        part = _dot(mixed.astype(BF16), wo_ref[cs, :])
        y = part if y is None else y + part
    m = mod_ref[0]
    x = _pick(lat_tiles, *x_refs) if split_x else x_refs[0][...]
    x1 = x + m[2:3] * y
    x1_ref[...] = x1
    u2_ref[...] = _norm_mod(x1, g2_ref[...], m[4:5], m[3:4]).astype(BF16)


def _merge(x_parts, u, h1, ya_lat, ya_ctx, s, w3, w_in, w_o, mods, g2, l, mod_row):
    n, d = u.shape
    tm = TM_MERGE
    lat_tiles = ya_lat.shape[0] // tm
    split_x = len(x_parts) == 2
    gate_off = (w_in.shape[2] - 3 * d) // d
    tile = pl.BlockSpec((tm, d), lambda i: (i, 0))
    split = _split_specs(tm, lat_tiles)
    gate_specs = [_resident((None, d, d), functools.partial(lambda b, i: (l, 0, gate_off + b), b)) for b in range(3)]
    return pl.pallas_call(
        functools.partial(_merge_kernel, lat_tiles=lat_tiles, split_x=split_x),
        grid=(n // tm,),
        in_specs=(split if split_x else [tile]) + [tile, tile] + split + [tile]
                 + [_resident((None, 3, d, d), lambda i: (l, 0, 0, 0))] + gate_specs
                 + [_resident((None, d, d), lambda i: (l, 0, 0)),
                    pl.BlockSpec((1, 6, d), lambda i: (mod_row(i, tm), 0, 0)),
                    pl.BlockSpec((1, d), lambda i: (0, 0))],
        out_specs=[tile, tile],
        out_shape=[jax.ShapeDtypeStruct((n, d), F32), jax.ShapeDtypeStruct((n, d), BF16)],
        compiler_params=_params("parallel"),
        name="merge",
    )(*x_parts, u, h1, ya_lat, ya_ctx, s, w3, w_in, w_in, w_in, w_o, mods, g2)


def _ffn_kernel(x_ref, u_ref, wg_ref, wu_ref, wd_ref, mod_ref, gn_ref, modn_ref, *rest, final):
    u = u_ref[...]
    f = None
    for c in range(wg_ref.shape[1] // MXU_COLS):
        cs = slice(c * MXU_COLS, (c + 1) * MXU_COLS)
        hidden = _silu(_dot(u, wg_ref[:, cs])) * _dot(u, wu_ref[:, cs])
        part = _dot(hidden.astype(BF16), wd_ref[cs, :])
        f = part if f is None else f + part
    x2 = x_ref[...] + mod_ref[0][5:6] * f
    if final:
        (y_ref,) = rest
        xn = x2 * lax.rsqrt(jnp.mean(x2 * x2, axis=-1, keepdims=True) + EPS)
        y_ref[...] = xn * gn_ref[...]
    else:
        x2_ref, un_ref = rest
        mn = modn_ref[0]
        x2_ref[...] = x2
        un_ref[...] = _norm_mod(x2, gn_ref[...], mn[1:2], mn[0:1]).astype(BF16)


def _ffn(x, u, wg, wu, wd, mods, g_next, mods_next, l, mod_row, final, row_off=0, rows=None):
    d = x.shape[1]
    n = x.shape[0] if rows is None else rows
    tm = TM_MERGE
    dff = wg.shape[2]
    assert dff % MXU_COLS == 0
    toff = row_off // tm
    tile = pl.BlockSpec((tm, d), lambda i: (i + toff, 0))
    out_tile = pl.BlockSpec((tm, d), lambda i: (i, 0))
    modspec = pl.BlockSpec((1, 6, d), lambda i: (mod_row(i + toff, tm), 0, 0))
    if final:
        out_shape = jax.ShapeDtypeStruct((n, d), F32)
        out_specs = out_tile
    else:
        out_shape = [jax.ShapeDtypeStruct((n, d), F32), jax.ShapeDtypeStruct((n, d), BF16)]
        out_specs = [out_tile, out_tile]
    return pl.pallas_call(
        functools.partial(_ffn_kernel, final=final),
        grid=(n // tm,),
        in_specs=[tile, tile,
                  _resident((None, d, dff), lambda i: (l, 0, 0)),
                  _resident((None, d, dff), lambda i: (l, 0, 0)),
                  _resident((None, dff, d), lambda i: (l, 0, 0)),
                  modspec,
                  pl.BlockSpec((1, d), lambda i: (0, 0)),
                  modspec],
        out_specs=out_specs,
        out_shape=out_shape,
        compiler_params=_params("parallel"),
        name="ffn_final" if final else "ffn",
    )(x, u, wg, wu, wd, mods, g_next, mods_next)


def kernel(x_prompt, x_sample, cache_k, cache_v, c, c_ctx, w_ada, b_ada, norm1_g, w_in, conv_dw_w, conv_dw_b, conv_ln_g, conv_ln_b, conv_pw_w, sc_dw_w, sc_out_w, na_rpb, na_out_w, w_o, norm2_g, ffn_w_gate, ffn_w_up, ffn_w_down, final_g):
    batch, seq, d = x_prompt.shape
    dec_batch, dec_seq, _ = x_sample.shape
    depth = w_ada.shape[0]
    n_lat = dec_batch * dec_seq
    n_ctx = batch * seq
    grid_rows = dec_seq // GRID_W
    assert d == D_MODEL and w_in.shape[2] == 11 * d
    assert seq == TM_CONV and dec_seq % TM_MM == 0 and n_ctx % TM_MM == 0
    assert grid_rows % NA_QROWS == 0 and grid_rows >= WIN_H and n_lat % TM_MERGE == 0

    def mod_row(i, tm):
        return jnp.minimum((i * tm) // dec_seq, dec_batch)

    n_cond = -(-(dec_batch + 1) // 8) * 8
    cond = jnp.zeros((n_cond, d), F32).at[:dec_batch].set(c).at[dec_batch].set(c_ctx)
    mods = _adaln(cond, w_ada, b_ada).reshape(depth, n_cond, 6, d)

    w_in_b = w_in.astype(BF16)
    w3_b = jnp.stack([conv_pw_w, na_out_w, sc_out_w], axis=1).astype(BF16)
    w_o_b = w_o.astype(BF16)
    wg_b, wu_b, wd_b = ffn_w_gate.astype(BF16), ffn_w_up.astype(BF16), ffn_w_down.astype(BF16)
    bias = _na_bias_tables(na_rpb)

    x_parts = (x_sample.reshape(n_lat, d), x_prompt.reshape(n_ctx, d))
    u = _prenorm(*x_parts, norm1_g[0][None], mods[0], mod_row)
    u_ctx = []
    for l in range(depth):
        u_ctx.append(u[n_lat:])
        h0, q, k, v, t, bg = _proj(u, w_in_b, l)
        h1, s = _conv_mixers(h0, t, bg, conv_dw_w[l], conv_dw_b[l][None], conv_ln_g[l][None],
                             conv_ln_b[l][None], sc_dw_w[l], n_lat, dec_seq)
        ya_lat = _attn_lat(q, k, v, cache_k, cache_v, bias, l, dec_batch, dec_seq)
        ya_ctx = _attn_ctx(q, k, v, n_lat, batch, seq)
        x, u2 = _merge(x_parts, u, h1, ya_lat, ya_ctx, s, w3_b, w_in_b, w_o_b, mods[l], norm2_g[l][None],
                       l, mod_row)
        if l + 1 < depth:
            x, u = _ffn(x, u2, wg_b, wu_b, wd_b, mods[l], norm1_g[l + 1][None], mods[l + 1], l, mod_row, False)
            x_parts = (x,)
        else:
            ffn_args = (x, u2, wg_b, wu_b, wd_b, mods[l], final_g[None], mods[l], l, mod_row, True)
            y_sample = _ffn(*ffn_args, row_off=0, rows=n_lat)
            y_prompt = _ffn(*ffn_args, row_off=n_lat, rows=n_ctx)
    new_kv = _proj_kv_ctx(jnp.stack(u_ctx, axis=0), w_in_b, batch, seq)

    return (y_prompt.reshape(batch, seq, d), y_sample.reshape(dec_batch, dec_seq, d), new_kv[0], new_kv[1])
```

```python
import functools

import numpy as np
import jax
import jax.numpy as jnp
from jax import lax
from jax.experimental import pallas as pl
from jax.experimental.pallas import tpu as pltpu

F32 = jnp.float32
BF16 = jnp.bfloat16

D_MODEL = 1024
HEADS = 16
HEAD_DIM = 64
GRID_W = 64
WIN_H = 8
WIN_W = 16
CONV_K = 31
SC_K = 3
EPS = 1e-6
LANES = 128
SUBLANES = 8
MXU_COLS = 256
HALO = 16
MASK_VALUE = -1e30

TM_MM = 1024
TM_MERGE = 512
TM_CONV = 256
TN_PROJ = 512
NA_QROWS = 8
NA_HEADS = 4
CTX_LANES = 512
NA_BAND_LANES = 1024
VMEM_LIMIT = 48 * 1024 * 1024


def _dot(a, b):
    return jnp.dot(a, b, preferred_element_type=F32)


def _dot_nt(a, b):
    return lax.dot_general(a, b, (((1,), (1,)), ((), ())), preferred_element_type=F32)


def _silu(x):
    return x * jax.nn.sigmoid(x)


def _norm_mod(x, g, scale, shift):
    xn = x * lax.rsqrt(jnp.mean(x * x, axis=-1, keepdims=True) + EPS)
    return (xn * g) * (1.0 + scale) + shift


def _params(*sem):
    return pltpu.CompilerParams(dimension_semantics=sem, vmem_limit_bytes=VMEM_LIMIT)


def _resident(block_shape, index_map):
    return pl.BlockSpec(block_shape, index_map, pipeline_mode=pl.Buffered(1))


def _adaln_kernel(cond_ref, w_ref, b_ref, o_ref):
    a = _silu(cond_ref[...]).astype(BF16)
    o_ref[...] = _dot(a, w_ref[...].astype(BF16)) + b_ref[...]


def _adaln(cond, w_ada, b_ada):
    depth, d, n = w_ada.shape
    rows = cond.shape[0]
    tn = 1536
    return pl.pallas_call(
        _adaln_kernel,
        grid=(depth, n // tn),
        in_specs=[pl.BlockSpec((rows, d), lambda l, j: (0, 0)),
                  pl.BlockSpec((None, d, tn), lambda l, j: (l, 0, j)),
                  pl.BlockSpec((None, 1, tn), lambda l, j: (l, 0, j))],
        out_specs=pl.BlockSpec((None, rows, tn), lambda l, j: (l, 0, j)),
        out_shape=jax.ShapeDtypeStruct((depth, rows, n), F32),
        compiler_params=_params("parallel", "parallel"),
        name="adaln",
    )(cond, w_ada, b_ada.reshape(depth, 1, n))


def _split_specs(tm, lat_tiles):
    lat = pl.BlockSpec((tm, D_MODEL), lambda i: (jnp.minimum(i, lat_tiles - 1), 0))
    ctx = pl.BlockSpec((tm, D_MODEL), lambda i: (jnp.maximum(i - lat_tiles, 0), 0))
    return [lat, ctx]


def _pick(lat_tiles, lat_ref, ctx_ref):
    return jnp.where(pl.program_id(0) < lat_tiles, lat_ref[...], ctx_ref[...])


def _prenorm_kernel(xl_ref, xc_ref, g_ref, mod_ref, u_ref, *, lat_tiles):
    m = mod_ref[0]
    u_ref[...] = _norm_mod(_pick(lat_tiles, xl_ref, xc_ref), g_ref[...], m[1:2], m[0:1]).astype(BF16)


def _prenorm(x_lat, x_ctx, g, mods, mod_row):
    d = x_lat.shape[1]
    n = x_lat.shape[0] + x_ctx.shape[0]
    lat_tiles = x_lat.shape[0] // TM_MM
    return pl.pallas_call(
        functools.partial(_prenorm_kernel, lat_tiles=lat_tiles),
        grid=(n // TM_MM,),
        in_specs=_split_specs(TM_MM, lat_tiles)
                 + [pl.BlockSpec((1, d), lambda i: (0, 0)),
                    pl.BlockSpec((1, 6, d), lambda i: (mod_row(i, TM_MM), 0, 0))],
        out_specs=pl.BlockSpec((TM_MM, d), lambda i: (i, 0)),
        out_shape=jax.ShapeDtypeStruct((n, d), BF16),
        compiler_params=_params("parallel"),
        name="prenorm",
    )(x_lat, x_ctx, g, mods)


def _proj_kernel(u_ref, w_ref, h0_ref, q_ref, k_ref, v_ref, t_ref, bg_ref):
    u = u_ref[...]

    def seg(n, cs):
        return _dot(u, w_ref[:, n * D_MODEL + cs.start:n * D_MODEL + cs.stop])

    for c in range(D_MODEL // MXU_COLS):
        cs = slice(c * MXU_COLS, (c + 1) * MXU_COLS)
        h0_ref[:, cs] = (seg(0, cs) * jax.nn.sigmoid(seg(1, cs))).astype(BF16)
        q_ref[:, cs] = (seg(2, cs) * (HEAD_DIM ** -0.5)).astype(BF16)
        k_ref[:, cs] = seg(3, cs).astype(BF16)
        v_ref[:, cs] = seg(4, cs).astype(BF16)
        bg_ref[:, cs] = seg(5, cs).astype(BF16)
        t_ref[:, cs] = (seg(6, cs) * seg(7, cs)).astype(BF16)


def _proj(u, w_in, l):
    n, d = u.shape
    tm = TM_MERGE
    tile = pl.BlockSpec((tm, d), lambda i: (i, 0))
    return pl.pallas_call(
        _proj_kernel,
        grid=(n // tm,),
        in_specs=[tile, _resident((None, d, 8 * d), lambda i: (l, 0, 0))],
        out_specs=[tile] * 6,
        out_shape=[jax.ShapeDtypeStruct((n, d), BF16)] * 6,
        compiler_params=_params("parallel"),
        name="proj",
    )(u, w_in)


def _proj_kv_ctx_kernel(u_ref, wk_ref, wv_ref, k_ref, v_ref):
    seq = k_ref.shape[2]
    u = u_ref[...]
    rk = _dot(u, wk_ref[...])
    rv = _dot(u, wv_ref[...])
    for b in range(k_ref.shape[0]):
        for h in range(TN_PROJ // HEAD_DIM):
            sl = slice(h * HEAD_DIM, (h + 1) * HEAD_DIM)
            k_ref[b, h] = rk[b * seq:(b + 1) * seq, sl]
            v_ref[b, h] = rv[b * seq:(b + 1) * seq, sl]


def _proj_kv_ctx(u_ctx, w_in, batch, seq):
    depth = u_ctx.shape[0]
    heads_per = TN_PROJ // HEAD_DIM
    koff, voff = 3 * D_MODEL // TN_PROJ, 4 * D_MODEL // TN_PROJ
    nb = TM_MERGE // seq
    out_spec = pl.BlockSpec((nb, None, heads_per, seq, HEAD_DIM), lambda l, j, b: (b, l, j, 0, 0))
    out_sds = jax.ShapeDtypeStruct((batch, depth, HEADS, seq, HEAD_DIM), F32)
    return pl.pallas_call(
        _proj_kv_ctx_kernel,
        grid=(depth, D_MODEL // TN_PROJ, batch // nb),
        in_specs=[pl.BlockSpec((None, nb * seq, D_MODEL), lambda l, j, b: (l, b, 0)),
                  pl.BlockSpec((None, D_MODEL, TN_PROJ), lambda l, j, b: (l, 0, j + koff)),
                  pl.BlockSpec((None, D_MODEL, TN_PROJ), lambda l, j, b: (l, 0, j + voff))],
        out_specs=[out_spec, out_spec],
        out_shape=[out_sds, out_sds],
        compiler_params=_params("parallel", "parallel", "arbitrary"),
        name="proj_kv_ctx",
    )(u_ctx, w_in, w_in)


def _fill_window(win_ref, prev_ref, cur_ref, next_ref, first, last, tm):
    prev = prev_ref[...].astype(F32)
    nxt = next_ref[...].astype(F32)
    win_ref[0:HALO, :] = jnp.where(first, 0.0, prev)
    win_ref[HALO:HALO + tm, :] = cur_ref[...].astype(F32)
    win_ref[HALO + tm:HALO + tm + HALO, :] = jnp.where(last, 0.0, nxt)


def _depthwise(win_ref, sh_ref, w_ref, out_ref, taps, tm):
    base = HALO - (taps - 1) // 2
    rows = 64
    sh_rows = sh_ref.shape[1]
    shifts = sorted({(base + k) % SUBLANES for k in range(taps)} - {0})

    def body(cb, carry):
        col = pl.multiple_of(cb * LANES, LANES)
        for s in shifts:
            sh_ref[s] = win_ref[pl.ds(s, sh_rows), pl.ds(col, LANES)]
        for r0 in range(0, tm, rows):
            acc = None
            for k in range(taps):
                off = base + r0 + k
                s, aligned = off % SUBLANES, off - off % SUBLANES
                if s == 0:
                    xk = win_ref[pl.ds(aligned, rows), pl.ds(col, LANES)]
                else:
                    xk = sh_ref[s, pl.ds(aligned, rows), :]
                term = xk * w_ref[pl.ds(k, 1), pl.ds(col, LANES)]
                acc = term if acc is None else acc + term
            out_ref[pl.ds(r0, rows), pl.ds(col, LANES)] = acc
        return carry

    lax.fori_loop(0, D_MODEL // LANES, body, 0)


def _conv_kernel(hp_ref, hc_ref, hn_ref, tp_ref, tc_ref, tn_ref, bg_ref,
                 wdw_ref, bdw_ref, lng_ref, lnb_ref, wsc_ref,
                 h1_ref, s_ref, win_ref, sh_ref, acc_ref, *, lat_tiles, lat_tiles_per_seq):
    tm = TM_CONV
    i = pl.program_id(0)
    is_lat = i < lat_tiles
    pos = i % lat_tiles_per_seq
    first = jnp.logical_or(jnp.logical_not(is_lat), pos == 0)
    last = jnp.logical_or(jnp.logical_not(is_lat), pos == lat_tiles_per_seq - 1)

    _fill_window(win_ref, hp_ref, hc_ref, hn_ref, first, last, tm)
    _depthwise(win_ref, sh_ref, wdw_ref, acc_ref, CONV_K, tm)
    y = acc_ref[...] + bdw_ref[...]
    mu = jnp.mean(y, axis=-1, keepdims=True)
    dlt = y - mu
    var = jnp.mean(dlt * dlt, axis=-1, keepdims=True)
    h = (dlt * lax.rsqrt(var + EPS)) * lng_ref[...] + lnb_ref[...]
    h1_ref[...] = _silu(h).astype(BF16)

    _fill_window(win_ref, tp_ref, tc_ref, tn_ref, first, last, tm)
    _depthwise(win_ref, sh_ref, wsc_ref, acc_ref, SC_K, tm)
    s_ref[...] = (bg_ref[...].astype(F32) * acc_ref[...]).astype(BF16)


def _conv_mixers(h0, t, bg, wdw, bdw, lng, lnb, wsc, lat_tokens, lat_seq):
    n, d = h0.shape
    tm = TM_CONV
    per = tm // HALO
    nblk = n // HALO
    cur = pl.BlockSpec((tm, d), lambda i: (i, 0))
    prev = pl.BlockSpec((HALO, d), lambda i: (jnp.maximum(i * per - 1, 0), 0))
    nxt = pl.BlockSpec((HALO, d), lambda i: (jnp.minimum((i + 1) * per, nblk - 1), 0))
    vec = pl.BlockSpec((1, d), lambda i: (0, 0))
    kern = functools.partial(_conv_kernel, lat_tiles=lat_tokens // tm, lat_tiles_per_seq=lat_seq // tm)
    return pl.pallas_call(
        kern,
        grid=(n // tm,),
        in_specs=[prev, cur, nxt, prev, cur, nxt, cur,
                  pl.BlockSpec((CONV_K, d), lambda i: (0, 0)), vec, vec, vec,
                  pl.BlockSpec((SC_K, d), lambda i: (0, 0))],
        out_specs=[cur, cur],
        out_shape=[jax.ShapeDtypeStruct((n, d), BF16)] * 2,
        scratch_shapes=[pltpu.VMEM((tm + 2 * HALO, d), F32),
                        pltpu.VMEM((SUBLANES, tm + 2 * HALO - SUBLANES, LANES), F32),
                        pltpu.VMEM((tm, d), F32)],
        compiler_params=_params("parallel"),
        name="conv_mixers",
    )(h0, h0, h0, t, t, t, bg, wdw, bdw, lng, lnb, wsc)


def _fold_lanes(op, *arrays):
    tiles = [a[:, i * LANES:(i + 1) * LANES] for a in arrays for i in range(a.shape[1] // LANES)]
    return functools.reduce(op, tiles)


def _attn_ctx_kernel(q_ref, k_ref, v_ref, o_ref):
    for p in range(CTX_LANES // LANES):
        outs = []
        for h in range(LANES // HEAD_DIM):
            sl = slice(p * LANES + h * HEAD_DIM, p * LANES + (h + 1) * HEAD_DIM)
            s = _dot_nt(q_ref[:, sl], k_ref[:, sl])
            e = jnp.exp(s - jnp.max(_fold_lanes(jnp.maximum, s), axis=-1, keepdims=True))
            den = jnp.sum(_fold_lanes(jnp.add, e), axis=-1, keepdims=True)
            outs.append(_dot(e.astype(BF16), v_ref[:, sl]) / den)
        o_ref[:, p * LANES:(p + 1) * LANES] = jnp.concatenate(outs, axis=-1).astype(BF16)


def _attn_ctx(q, k, v, lat_tokens, batch, seq):
    roff = lat_tokens // seq
    spec = pl.BlockSpec((seq, CTX_LANES), lambda b, hg: (b + roff, hg))
    return pl.pallas_call(
        _attn_ctx_kernel,
        grid=(batch, D_MODEL // CTX_LANES),
        in_specs=[spec, spec, spec],
        out_specs=pl.BlockSpec((seq, CTX_LANES), lambda b, hg: (b, hg)),
        out_shape=jax.ShapeDtypeStruct((batch * seq, D_MODEL), BF16),
        compiler_params=_params("parallel", "parallel"),
        name="attn_ctx",
    )(q, k, v)


def _attn_lat_kernel(q_ref, k_ref, v_ref, kct_ref, vct_ref, bias_ref, o_ref, ko_ref, vo_ref, *, grid_rows):
    rb = pl.program_id(2)
    nloc = WIN_H * GRID_W
    per_tile = LANES // HEAD_DIM

    @pl.when(rb == 0)
    def _():
        for p in range(NA_HEADS // per_tile):
            ko_ref[p] = k_ref[:, p * LANES + HEAD_DIM:(p + 1) * LANES]
            vo_ref[p] = v_ref[:, p * LANES + HEAD_DIM:(p + 1) * LANES]

    def window(ref, odd_ref, h, start):
        if h % per_tile == 0:
            return ref[pl.ds(start, nloc), h * HEAD_DIM:(h + 1) * HEAD_DIM]
        return odd_ref[h // per_tile, pl.ds(start, nloc), :]

    starts, band_sel, band_off = [], [], []
    for j in range(NA_QROWS):
        r = rb * NA_QROWS + j
        rs = jnp.clip(r - WIN_H // 2, 0, grid_rows - WIN_H)
        starts.append(pl.multiple_of(rs * GRID_W, GRID_W))
        lane0 = (WIN_H - 1 - (r - rs)) * GRID_W
        band_sel.append((lane0 // GRID_W) % (LANES // GRID_W))
        band_off.append(pl.multiple_of((lane0 // LANES) * LANES, LANES))

    scores = []
    for h in range(NA_HEADS):
        q = q_ref[:, h * HEAD_DIM:(h + 1) * HEAD_DIM]
        s_ctx = _dot(q, kct_ref[h].astype(BF16))
        s_rows = []
        for j in range(NA_QROWS):
            bias = bias_ref[h, band_sel[j], :, pl.ds(band_off[j], nloc)]
            s_rows.append(_dot_nt(q[j * GRID_W:(j + 1) * GRID_W], window(k_ref, ko_ref, h, starts[j])) + bias)
        scores.append((jnp.concatenate(s_rows, axis=0), s_ctx))
    probs = []
    for s_loc, s_ctx in scores:
        m = jnp.max(_fold_lanes(jnp.maximum, s_loc, s_ctx), axis=-1, keepdims=True)
        e_loc = jnp.exp(s_loc - m)
        e_ctx = jnp.exp(s_ctx - m)
        den = jnp.sum(_fold_lanes(jnp.add, e_loc, e_ctx), axis=-1, keepdims=True)
        probs.append((e_loc.astype(BF16), e_ctx.astype(BF16), den))
    outs = []
    for h in range(NA_HEADS):
        p_loc, p_ctx, den = probs[h]
        o_rows = [_dot(p_loc[j * GRID_W:(j + 1) * GRID_W], window(v_ref, vo_ref, h, starts[j]))
                  for j in range(NA_QROWS)]
        o = jnp.concatenate(o_rows, axis=0) + _dot_nt(p_ctx, vct_ref[h].astype(BF16))
        outs.append(o / den)
    o_ref[...] = jnp.concatenate(outs, axis=-1).astype(BF16)


def _attn_lat(q, k, v, kct, vct, bias, l, dec_batch, dec_seq):
    grid_rows = dec_seq // GRID_W
    nrb = grid_rows // NA_QROWS
    tq = NA_QROWS * GRID_W
    width = NA_HEADS * HEAD_DIM
    past = kct.shape[4]
    seq_spec = pl.BlockSpec((dec_seq, width), lambda hg, b, rb: (b, hg))
    ctx_spec = pl.BlockSpec((None, None, NA_HEADS, HEAD_DIM, past), lambda hg, b, rb: (b, l, hg, 0, 0))
    q_spec = pl.BlockSpec((tq, width), lambda hg, b, rb: (b * nrb + rb, hg))
    return pl.pallas_call(
        functools.partial(_attn_lat_kernel, grid_rows=grid_rows),
        grid=(D_MODEL // width, dec_batch, nrb),
        in_specs=[q_spec, seq_spec, seq_spec, ctx_spec, ctx_spec,
                  pl.BlockSpec((None, NA_HEADS, LANES // GRID_W, GRID_W, NA_BAND_LANES),
                               lambda hg, b, rb: (l, hg, 0, 0, 0))],
        out_specs=q_spec,
        out_shape=jax.ShapeDtypeStruct((dec_batch * dec_seq, D_MODEL), BF16),
        scratch_shapes=[pltpu.VMEM((NA_HEADS * HEAD_DIM // LANES, dec_seq, HEAD_DIM), BF16)] * 2,
        compiler_params=_params("arbitrary", "arbitrary", "arbitrary"),
        name="attn_lat",
    )(q, k, v, kct, vct, bias)


def _na_bias_tables(rpb):
    c = np.arange(GRID_W)
    cs = np.clip(c - WIN_W // 2, 0, GRID_W - WIN_W)
    kc = np.arange(GRID_W)
    col_ok = (kc[None, :] >= cs[:, None]) & (kc[None, :] < cs[:, None] + WIN_W)
    dcol = kc[None, :] - c[:, None] + WIN_W - 1
    n_row, n_col = 2 * WIN_H - 1, 2 * WIN_W - 1
    band_rows = NA_BAND_LANES // GRID_W
    depth, heads = rpb.shape[0], rpb.shape[1]
    rpb2 = jnp.stack([jnp.pad(rpb[:, :, b:], ((0, 0), (0, 0), (0, band_rows - n_row + b), (0, 0)))
                      for b in range(LANES // GRID_W)], axis=2)
    onehot = (np.arange(n_col)[:, None, None] == dcol[None]).astype(np.float32)
    band = jnp.einsum('lhbid,dck->lhbcik', rpb2, onehot, precision=lax.Precision.HIGHEST)
    band = jnp.where(col_ok[None, None, None, :, None, :], band, MASK_VALUE)
    return band.reshape(depth, heads, LANES // GRID_W, GRID_W, NA_BAND_LANES)


def _merge_kernel(*refs, lat_tiles, split_x):
    x_refs, refs = refs[:1 + split_x], refs[1 + split_x:]
    (u_ref, h1_ref, yl_ref, yc_ref, s_ref, w3_ref, wgc_ref, wgn_ref, wgs_ref, wo_ref, mod_ref, g2_ref,
     x1_ref, u2_ref) = refs
    u = u_ref[...]
    branches = ((h1_ref[...], wgc_ref), (_pick(lat_tiles, yl_ref, yc_ref), wgn_ref), (s_ref[...], wgs_ref))
    y = None
    for c in range(D_MODEL // MXU_COLS):
        cs = slice(c * MXU_COLS, (c + 1) * MXU_COLS)
        mixed = None
        for b, (inp, wg_ref) in enumerate(branches):
            term = jax.nn.sigmoid(_dot(u, wg_ref[:, cs])) * _dot(inp, w3_ref[b, :, cs])
            mixed = term if mixed is None else mixed + term
        part = _dot(mixed.astype(BF16), wo_ref[cs, :])
        y = part if y is None else y + part
    m = mod_ref[0]
    x = _pick(lat_tiles, *x_refs) if split_x else x_refs[0][...]
    x1 = x + m[2:3] * y
    x1_ref[...] = x1
    u2_ref[...] = _norm_mod(x1, g2_ref[...], m[4:5], m[3:4]).astype(BF16)


def _merge(x_parts, u, h1, ya_lat, ya_ctx, s, w3, w_in, w_o, mods, g2, l, mod_row):
    n, d = u.shape
    tm = TM_MERGE
    lat_tiles = ya_lat.shape[0] // tm
    split_x = len(x_parts) == 2
    gate_off = (w_in.shape[2] - 3 * d) // d
    tile = pl.BlockSpec((tm, d), lambda i: (i, 0))
    split = _split_specs(tm, lat_tiles)
    gate_specs = [_resident((None, d, d), functools.partial(lambda b, i: (l, 0, gate_off + b), b)) for b in range(3)]
    return pl.pallas_call(
        functools.partial(_merge_kernel, lat_tiles=lat_tiles, split_x=split_x),
        grid=(n // tm,),
        in_specs=(split if split_x else [tile]) + [tile, tile] + split + [tile]
                 + [_resident((None, 3, d, d), lambda i: (l, 0, 0, 0))] + gate_specs
                 + [_resident((None, d, d), lambda i: (l, 0, 0)),
                    pl.BlockSpec((1, 6, d), lambda i: (mod_row(i, tm), 0, 0)),
                    pl.BlockSpec((1, d), lambda i: (0, 0))],
        out_specs=[tile, tile],
        out_shape=[jax.ShapeDtypeStruct((n, d), F32), jax.ShapeDtypeStruct((n, d), BF16)],
        compiler_params=_params("parallel"),
        name="merge",
    )(*x_parts, u, h1, ya_lat, ya_ctx, s, w3, w_in, w_in, w_in, w_o, mods, g2)


def _ffn_kernel(x_ref, u_ref, wg_ref, wu_ref, wd_ref, mod_ref, gn_ref, modn_ref, *rest, final):
    u = u_ref[...]
    f = None
    for c in range(wg_ref.shape[1] // MXU_COLS):
        cs = slice(c * MXU_COLS, (c + 1) * MXU_COLS)
        hidden = _silu(_dot(u, wg_ref[:, cs])) * _dot(u, wu_ref[:, cs])
        part = _dot(hidden.astype(BF16), wd_ref[cs, :])
        f = part if f is None else f + part
    x2 = x_ref[...] + mod_ref[0][5:6] * f
    if final:
        (y_ref,) = rest
        xn = x2 * lax.rsqrt(jnp.mean(x2 * x2, axis=-1, keepdims=True) + EPS)
        y_ref[...] = xn * gn_ref[...]
    else:
        x2_ref, un_ref = rest
        mn = modn_ref[0]
        x2_ref[...] = x2
        un_ref[...] = _norm_mod(x2, gn_ref[...], mn[1:2], mn[0:1]).astype(BF16)


def _ffn(x, u, wg, wu, wd, mods, g_next, mods_next, l, mod_row, final, row_off=0, rows=None):
    d = x.shape[1]
    n = x.shape[0] if rows is None else rows
    tm = TM_MERGE
    dff = wg.shape[2]
    assert dff % MXU_COLS == 0
    toff = row_off // tm
    tile = pl.BlockSpec((tm, d), lambda i: (i + toff, 0))
    out_tile = pl.BlockSpec((tm, d), lambda i: (i, 0))
    modspec = pl.BlockSpec((1, 6, d), lambda i: (mod_row(i + toff, tm), 0, 0))
    if final:
        out_shape = jax.ShapeDtypeStruct((n, d), F32)
        out_specs = out_tile
    else:
        out_shape = [jax.ShapeDtypeStruct((n, d), F32), jax.ShapeDtypeStruct((n, d), BF16)]
        out_specs = [out_tile, out_tile]
    return pl.pallas_call(
        functools.partial(_ffn_kernel, final=final),
        grid=(n // tm,),
        in_specs=[tile, tile,
                  _resident((None, d, dff), lambda i: (l, 0, 0)),
                  _resident((None, d, dff), lambda i: (l, 0, 0)),
                  _resident((None, dff, d), lambda i: (l, 0, 0)),
                  modspec,
                  pl.BlockSpec((1, d), lambda i: (0, 0)),
                  modspec],
        out_specs=out_specs,
        out_shape=out_shape,
        compiler_params=_params("parallel"),
        name="ffn_final" if final else "ffn",
    )(x, u, wg, wu, wd, mods, g_next, mods_next)


def kernel(x_prompt, x_sample, cache_k, cache_v, c, c_ctx, w_ada, b_ada, norm1_g, w_in, conv_dw_w, conv_dw_b, conv_ln_g, conv_ln_b, conv_pw_w, sc_dw_w, sc_out_w, na_rpb, na_out_w, w_o, norm2_g, ffn_w_gate, ffn_w_up, ffn_w_down, final_g):
    batch, seq, d = x_prompt.shape
    dec_batch, dec_seq, _ = x_sample.shape
    depth = w_ada.shape[0]
    n_lat = dec_batch * dec_seq
    n_ctx = batch * seq
    grid_rows = dec_seq // GRID_W
    assert d == D_MODEL and w_in.shape[2] == 11 * d
    assert seq == TM_CONV and dec_seq % TM_MM == 0 and n_ctx % TM_MM == 0
    assert grid_rows % NA_QROWS == 0 and grid_rows >= WIN_H and n_lat % TM_MERGE == 0

    def mod_row(i, tm):
        return jnp.minimum((i * tm) // dec_seq, dec_batch)

    n_cond = -(-(dec_batch + 1) // 8) * 8
    cond = jnp.zeros((n_cond, d), F32).at[:dec_batch].set(c).at[dec_batch].set(c_ctx)
    mods = _adaln(cond, w_ada, b_ada).reshape(depth, n_cond, 6, d)

    w_in_b = w_in.astype(BF16)
    w3_b = jnp.stack([conv_pw_w, na_out_w, sc_out_w], axis=1).astype(BF16)
    w_o_b = w_o.astype(BF16)
    wg_b, wu_b, wd_b = ffn_w_gate.astype(BF16), ffn_w_up.astype(BF16), ffn_w_down.astype(BF16)
    bias = _na_bias_tables(na_rpb)
    kct, vct = jnp.swapaxes(cache_k, 3, 4), jnp.swapaxes(cache_v, 3, 4)

    x_parts = (x_sample.reshape(n_lat, d), x_prompt.reshape(n_ctx, d))
    u = _prenorm(*x_parts, norm1_g[0][None], mods[0], mod_row)
    u_ctx = []
    for l in range(depth):
        u_ctx.append(u[n_lat:])
        h0, q, k, v, t, bg = _proj(u, w_in_b, l)
        h1, s = _conv_mixers(h0, t, bg, conv_dw_w[l], conv_dw_b[l][None], conv_ln_g[l][None],
                             conv_ln_b[l][None], sc_dw_w[l], n_lat, dec_seq)
        ya_lat = _attn_lat(q, k, v, kct, vct, bias, l, dec_batch, dec_seq)
        ya_ctx = _attn_ctx(q, k, v, n_lat, batch, seq)
        x, u2 = _merge(x_parts, u, h1, ya_lat, ya_ctx, s, w3_b, w_in_b, w_o_b, mods[l], norm2_g[l][None],
                       l, mod_row)
        if l + 1 < depth:
            x, u = _ffn(x, u2, wg_b, wu_b, wd_b, mods[l], norm1_g[l + 1][None], mods[l + 1], l, mod_row, False)
            x_parts = (x,)
        else:
            ffn_args = (x, u2, wg_b, wu_b, wd_b, mods[l], final_g[None], mods[l], l, mod_row, True)
            y_sample = _ffn(*ffn_args, row_off=0, rows=n_lat)
            y_prompt = _ffn(*ffn_args, row_off=n_lat, rows=n_ctx)
    new_kv = _proj_kv_ctx(jnp.stack(u_ctx, axis=0), w_in_b, batch, seq)

    return (y_prompt.reshape(batch, seq, d), y_sample.reshape(dec_batch, dec_seq, d), new_kv[0], new_kv[1])
```

```python
import functools

import numpy as np
import jax
import jax.numpy as jnp
from jax import lax
from jax.experimental import pallas as pl
from jax.experimental.pallas import tpu as pltpu

F32 = jnp.float32
BF16 = jnp.bfloat16

D_MODEL = 1024
HEADS = 16
HEAD_DIM = 64
GRID_W = 64
WIN_H = 8
WIN_W = 16
CONV_K = 31
SC_K = 3
EPS = 1e-6
LANES = 128
SUBLANES = 8
MXU_COLS = 256
HALO = 16
MASK_VALUE = -1e30
LOG2E = 1.4426950408889634

TM_MM = 1024
TM_MERGE = 512
TM_CONV = 256
TN_PROJ = 512
NA_QROWS = 8
NA_HEADS = 4
CTX_LANES = 512
NA_BAND_LANES = 1024
VMEM_LIMIT = 48 * 1024 * 1024


def _dot(a, b):
    return jnp.dot(a, b, preferred_element_type=F32)


def _dot_nt(a, b):
    return lax.dot_general(a, b, (((1,), (1,)), ((), ())), preferred_element_type=F32)


def _silu(x):
    return x * jax.nn.sigmoid(x)


def _norm_mod(x, g, scale, shift):
    xn = x * lax.rsqrt(jnp.mean(x * x, axis=-1, keepdims=True) + EPS)
    return (xn * g) * (1.0 + scale) + shift


def _params(*sem):
    return pltpu.CompilerParams(dimension_semantics=sem, vmem_limit_bytes=VMEM_LIMIT)


def _resident(block_shape, index_map):
    return pl.BlockSpec(block_shape, index_map, pipeline_mode=pl.Buffered(1))


def _adaln_kernel(cond_ref, w_ref, b_ref, o_ref):
    a = _silu(cond_ref[...]).astype(BF16)
    o_ref[...] = _dot(a, w_ref[...].astype(BF16)) + b_ref[...]


def _adaln(cond, w_ada, b_ada):
    depth, d, n = w_ada.shape
    rows = cond.shape[0]
    tn = 1536
    return pl.pallas_call(
        _adaln_kernel,
        grid=(depth, n // tn),
        in_specs=[pl.BlockSpec((rows, d), lambda l, j: (0, 0)),
                  pl.BlockSpec((None, d, tn), lambda l, j: (l, 0, j)),
                  pl.BlockSpec((None, 1, tn), lambda l, j: (l, 0, j))],
        out_specs=pl.BlockSpec((None, rows, tn), lambda l, j: (l, 0, j)),
        out_shape=jax.ShapeDtypeStruct((depth, rows, n), F32),
        compiler_params=_params("parallel", "parallel"),
        name="adaln",
    )(cond, w_ada, b_ada.reshape(depth, 1, n))


def _split_specs(tm, lat_tiles):
    lat = pl.BlockSpec((tm, D_MODEL), lambda i: (jnp.minimum(i, lat_tiles - 1), 0))
    ctx = pl.BlockSpec((tm, D_MODEL), lambda i: (jnp.maximum(i - lat_tiles, 0), 0))
    return [lat, ctx]


def _pick(lat_tiles, lat_ref, ctx_ref):
    return jnp.where(pl.program_id(0) < lat_tiles, lat_ref[...], ctx_ref[...])


def _prenorm_kernel(xl_ref, xc_ref, g_ref, mod_ref, u_ref, *, lat_tiles):
    m = mod_ref[0]
    u_ref[...] = _norm_mod(_pick(lat_tiles, xl_ref, xc_ref), g_ref[...], m[1:2], m[0:1]).astype(BF16)


def _prenorm(x_lat, x_ctx, g, mods, mod_row):
    d = x_lat.shape[1]
    n = x_lat.shape[0] + x_ctx.shape[0]
    lat_tiles = x_lat.shape[0] // TM_MM
    return pl.pallas_call(
        functools.partial(_prenorm_kernel, lat_tiles=lat_tiles),
        grid=(n // TM_MM,),
        in_specs=_split_specs(TM_MM, lat_tiles)
                 + [pl.BlockSpec((1, d), lambda i: (0, 0)),
                    pl.BlockSpec((1, 6, d), lambda i: (mod_row(i, TM_MM), 0, 0))],
        out_specs=pl.BlockSpec((TM_MM, d), lambda i: (i, 0)),
        out_shape=jax.ShapeDtypeStruct((n, d), BF16),
        compiler_params=_params("parallel"),
        name="prenorm",
    )(x_lat, x_ctx, g, mods)


def _proj_kernel(u_ref, w_ref, h0_ref, q_ref, k_ref, v_ref, t_ref, bg_ref):
    u = u_ref[...]

    def seg(n, cs):
        return _dot(u, w_ref[:, n * D_MODEL + cs.start:n * D_MODEL + cs.stop])

    for c in range(D_MODEL // MXU_COLS):
        cs = slice(c * MXU_COLS, (c + 1) * MXU_COLS)
        h0_ref[:, cs] = (seg(0, cs) * jax.nn.sigmoid(seg(1, cs))).astype(BF16)
        q_ref[:, cs] = (seg(2, cs) * (HEAD_DIM ** -0.5 * LOG2E)).astype(BF16)
        k_ref[:, cs] = seg(3, cs).astype(BF16)
        v_ref[:, cs] = seg(4, cs).astype(BF16)
        bg_ref[:, cs] = seg(5, cs).astype(BF16)
        t_ref[:, cs] = (seg(6, cs) * seg(7, cs)).astype(BF16)


def _proj(u, w_in, l):
    n, d = u.shape
    tm = TM_MERGE
    tile = pl.BlockSpec((tm, d), lambda i: (i, 0))
    return pl.pallas_call(
        _proj_kernel,
        grid=(n // tm,),
        in_specs=[tile, _resident((None, d, 8 * d), lambda i: (l, 0, 0))],
        out_specs=[tile] * 6,
        out_shape=[jax.ShapeDtypeStruct((n, d), BF16)] * 6,
        compiler_params=_params("parallel"),
        name="proj",
    )(u, w_in)


def _proj_kv_ctx_kernel(u_ref, wk_ref, wv_ref, k_ref, v_ref):
    seq = k_ref.shape[2]
    u = u_ref[...]
    rk = _dot(u, wk_ref[...])
    rv = _dot(u, wv_ref[...])
    for b in range(k_ref.shape[0]):
        for h in range(TN_PROJ // HEAD_DIM):
            sl = slice(h * HEAD_DIM, (h + 1) * HEAD_DIM)
            k_ref[b, h] = rk[b * seq:(b + 1) * seq, sl]
            v_ref[b, h] = rv[b * seq:(b + 1) * seq, sl]


def _proj_kv_ctx(u_ctx, w_in, batch, seq):
    depth = u_ctx.shape[0]
    heads_per = TN_PROJ // HEAD_DIM
    koff, voff = 3 * D_MODEL // TN_PROJ, 4 * D_MODEL // TN_PROJ
    nb = TM_MERGE // seq
    out_spec = pl.BlockSpec((nb, None, heads_per, seq, HEAD_DIM), lambda l, j, b: (b, l, j, 0, 0))
    out_sds = jax.ShapeDtypeStruct((batch, depth, HEADS, seq, HEAD_DIM), F32)
    return pl.pallas_call(
        _proj_kv_ctx_kernel,
        grid=(depth, D_MODEL // TN_PROJ, batch // nb),
        in_specs=[pl.BlockSpec((None, nb * seq, D_MODEL), lambda l, j, b: (l, b, 0)),
                  pl.BlockSpec((None, D_MODEL, TN_PROJ), lambda l, j, b: (l, 0, j + koff)),
                  pl.BlockSpec((None, D_MODEL, TN_PROJ), lambda l, j, b: (l, 0, j + voff))],
        out_specs=[out_spec, out_spec],
        out_shape=[out_sds, out_sds],
        compiler_params=_params("parallel", "parallel", "arbitrary"),
        name="proj_kv_ctx",
    )(u_ctx, w_in, w_in)


def _fill_window(win_ref, prev_ref, cur_ref, next_ref, first, last, tm):
    prev = prev_ref[...].astype(F32)
    nxt = next_ref[...].astype(F32)
    win_ref[0:HALO, :] = jnp.where(first, 0.0, prev)
    win_ref[HALO:HALO + tm, :] = cur_ref[...].astype(F32)
    win_ref[HALO + tm:HALO + tm + HALO, :] = jnp.where(last, 0.0, nxt)


def _depthwise(win_ref, sh_ref, w_ref, out_ref, taps, tm):
    base = HALO - (taps - 1) // 2
    rows = 64
    sh_rows = sh_ref.shape[1]
    shifts = sorted({(base + k) % SUBLANES for k in range(taps)} - {0})

    def body(cb, carry):
        col = pl.multiple_of(cb * LANES, LANES)
        for s in shifts:
            sh_ref[s] = win_ref[pl.ds(s, sh_rows), pl.ds(col, LANES)]
        for r0 in range(0, tm, rows):
            acc = None
            for k in range(taps):
                off = base + r0 + k
                s, aligned = off % SUBLANES, off - off % SUBLANES
                if s == 0:
                    xk = win_ref[pl.ds(aligned, rows), pl.ds(col, LANES)]
                else:
                    xk = sh_ref[s, pl.ds(aligned, rows), :]
                term = xk * w_ref[pl.ds(k, 1), pl.ds(col, LANES)]
                acc = term if acc is None else acc + term
            out_ref[pl.ds(r0, rows), pl.ds(col, LANES)] = acc
        return carry

    lax.fori_loop(0, D_MODEL // LANES, body, 0)


def _conv_kernel(hp_ref, hc_ref, hn_ref, tp_ref, tc_ref, tn_ref, bg_ref,
                 wdw_ref, bdw_ref, lng_ref, lnb_ref, wsc_ref,
                 h1_ref, s_ref, win_ref, sh_ref, acc_ref, *, lat_tiles, lat_tiles_per_seq):
    tm = TM_CONV
    i = pl.program_id(0)
    is_lat = i < lat_tiles
    pos = i % lat_tiles_per_seq
    first = jnp.logical_or(jnp.logical_not(is_lat), pos == 0)
    last = jnp.logical_or(jnp.logical_not(is_lat), pos == lat_tiles_per_seq - 1)

    _fill_window(win_ref, hp_ref, hc_ref, hn_ref, first, last, tm)
    _depthwise(win_ref, sh_ref, wdw_ref, acc_ref, CONV_K, tm)
    y = acc_ref[...] + bdw_ref[...]
    mu = jnp.mean(y, axis=-1, keepdims=True)
    dlt = y - mu
    var = jnp.mean(dlt * dlt, axis=-1, keepdims=True)
    h = (dlt * lax.rsqrt(var + EPS)) * lng_ref[...] + lnb_ref[...]
    h1_ref[...] = _silu(h).astype(BF16)

    _fill_window(win_ref, tp_ref, tc_ref, tn_ref, first, last, tm)
    _depthwise(win_ref, sh_ref, wsc_ref, acc_ref, SC_K, tm)
    s_ref[...] = (bg_ref[...].astype(F32) * acc_ref[...]).astype(BF16)


def _conv_mixers(h0, t, bg, wdw, bdw, lng, lnb, wsc, lat_tokens, lat_seq):
    n, d = h0.shape
    tm = TM_CONV
    per = tm // HALO
    nblk = n // HALO
    cur = pl.BlockSpec((tm, d), lambda i: (i, 0))
    prev = pl.BlockSpec((HALO, d), lambda i: (jnp.maximum(i * per - 1, 0), 0))
    nxt = pl.BlockSpec((HALO, d), lambda i: (jnp.minimum((i + 1) * per, nblk - 1), 0))
    vec = pl.BlockSpec((1, d), lambda i: (0, 0))
    kern = functools.partial(_conv_kernel, lat_tiles=lat_tokens // tm, lat_tiles_per_seq=lat_seq // tm)
    return pl.pallas_call(
        kern,
        grid=(n // tm,),
        in_specs=[prev, cur, nxt, prev, cur, nxt, cur,
                  pl.BlockSpec((CONV_K, d), lambda i: (0, 0)), vec, vec, vec,
                  pl.BlockSpec((SC_K, d), lambda i: (0, 0))],
        out_specs=[cur, cur],
        out_shape=[jax.ShapeDtypeStruct((n, d), BF16)] * 2,
        scratch_shapes=[pltpu.VMEM((tm + 2 * HALO, d), F32),
                        pltpu.VMEM((SUBLANES, tm + 2 * HALO - SUBLANES, LANES), F32),
                        pltpu.VMEM((tm, d), F32)],
        compiler_params=_params("parallel"),
        name="conv_mixers",
    )(h0, h0, h0, t, t, t, bg, wdw, bdw, lng, lnb, wsc)


def _fold_lanes(op, *arrays):
    tiles = [a[:, i * LANES:(i + 1) * LANES] for a in arrays for i in range(a.shape[1] // LANES)]
    return functools.reduce(op, tiles)


def _attn_ctx_kernel(q_ref, k_ref, v_ref, o_ref):
    heads = [slice(h * HEAD_DIM, (h + 1) * HEAD_DIM) for h in range(CTX_LANES // HEAD_DIM)]
    scores = [_dot_nt(q_ref[:, sl], k_ref[:, sl]) for sl in heads]
    probs = []
    for s in scores:
        e = jnp.exp2(s - jnp.max(_fold_lanes(jnp.maximum, s), axis=-1, keepdims=True))
        probs.append((e.astype(BF16), jnp.sum(_fold_lanes(jnp.add, e), axis=-1, keepdims=True)))
    outs = [_dot(p, v_ref[:, sl]) / den for (p, den), sl in zip(probs, heads)]
    o_ref[...] = jnp.concatenate(outs, axis=-1).astype(BF16)


def _attn_ctx(q, k, v, lat_tokens, batch, seq):
    roff = lat_tokens // seq
    spec = pl.BlockSpec((seq, CTX_LANES), lambda b, hg: (b + roff, hg))
    return pl.pallas_call(
        _attn_ctx_kernel,
        grid=(batch, D_MODEL // CTX_LANES),
        in_specs=[spec, spec, spec],
        out_specs=pl.BlockSpec((seq, CTX_LANES), lambda b, hg: (b, hg)),
        out_shape=jax.ShapeDtypeStruct((batch * seq, D_MODEL), BF16),
        compiler_params=_params("parallel", "parallel"),
        name="attn_ctx",
    )(q, k, v)


def _attn_lat_kernel(q_ref, k_ref, v_ref, kct_ref, vct_ref, bias_ref, o_ref, ko_ref, vo_ref, *, grid_rows):
    rb = pl.program_id(2)
    nloc = WIN_H * GRID_W
    per_tile = LANES // HEAD_DIM

    @pl.when(rb == 0)
    def _():
        for p in range(NA_HEADS // per_tile):
            ko_ref[p] = k_ref[:, p * LANES + HEAD_DIM:(p + 1) * LANES]
            vo_ref[p] = v_ref[:, p * LANES + HEAD_DIM:(p + 1) * LANES]

    def window(ref, odd_ref, h, start):
        if h % per_tile == 0:
            return ref[pl.ds(start, nloc), h * HEAD_DIM:(h + 1) * HEAD_DIM]
        return odd_ref[h // per_tile, pl.ds(start, nloc), :]

    starts, band_sel, band_off = [], [], []
    for j in range(NA_QROWS):
        r = rb * NA_QROWS + j
        rs = jnp.clip(r - WIN_H // 2, 0, grid_rows - WIN_H)
        starts.append(pl.multiple_of(rs * GRID_W, GRID_W))
        lane0 = (WIN_H - 1 - (r - rs)) * GRID_W
        band_sel.append((lane0 // GRID_W) % (LANES // GRID_W))
        band_off.append(pl.multiple_of((lane0 // LANES) * LANES, LANES))

    scores = []
    for h in range(NA_HEADS):
        q = q_ref[:, h * HEAD_DIM:(h + 1) * HEAD_DIM]
        s_ctx = _dot(q, kct_ref[h].astype(BF16))
        s_rows = []
        for j in range(NA_QROWS):
            bias = bias_ref[h, band_sel[j], :, pl.ds(band_off[j], nloc)]
            s_rows.append(_dot_nt(q[j * GRID_W:(j + 1) * GRID_W], window(k_ref, ko_ref, h, starts[j])) + bias)
        scores.append((jnp.concatenate(s_rows, axis=0), s_ctx))
    probs = []
    for s_loc, s_ctx in scores:
        m = jnp.max(_fold_lanes(jnp.maximum, s_loc, s_ctx), axis=-1, keepdims=True)
        e_loc = jnp.exp2(s_loc - m)
        e_ctx = jnp.exp2(s_ctx - m)
        den = jnp.sum(_fold_lanes(jnp.add, e_loc, e_ctx), axis=-1, keepdims=True)
        probs.append((e_loc.astype(BF16), e_ctx.astype(BF16), den))
    outs = []
    for h in range(NA_HEADS):
        p_loc, p_ctx, den = probs[h]
        o_rows = [_dot(p_loc[j * GRID_W:(j + 1) * GRID_W], window(v_ref, vo_ref, h, starts[j]))
                  for j in range(NA_QROWS)]
        o = jnp.concatenate(o_rows, axis=0) + _dot_nt(p_ctx, vct_ref[h].astype(BF16))
        outs.append(o / den)
    o_ref[...] = jnp.concatenate(outs, axis=-1).astype(BF16)


def _attn_lat(q, k, v, kct, vct, bias, l, dec_batch, dec_seq):
    grid_rows = dec_seq // GRID_W
    nrb = grid_rows // NA_QROWS
    tq = NA_QROWS * GRID_W
    width = NA_HEADS * HEAD_DIM
    past = kct.shape[4]
    seq_spec = pl.BlockSpec((dec_seq, width), lambda hg, b, rb: (b, hg))
    ctx_spec = pl.BlockSpec((None, None, NA_HEADS, HEAD_DIM, past), lambda hg, b, rb: (b, l, hg, 0, 0))
    q_spec = pl.BlockSpec((tq, width), lambda hg, b, rb: (b * nrb + rb, hg))
    return pl.pallas_call(
        functools.partial(_attn_lat_kernel, grid_rows=grid_rows),
        grid=(D_MODEL // width, dec_batch, nrb),
        in_specs=[q_spec, seq_spec, seq_spec, ctx_spec, ctx_spec,
                  pl.BlockSpec((None, NA_HEADS, LANES // GRID_W, GRID_W, NA_BAND_LANES),
                               lambda hg, b, rb: (l, hg, 0, 0, 0))],
        out_specs=q_spec,
        out_shape=jax.ShapeDtypeStruct((dec_batch * dec_seq, D_MODEL), BF16),
        scratch_shapes=[pltpu.VMEM((NA_HEADS * HEAD_DIM // LANES, dec_seq, HEAD_DIM), BF16)] * 2,
        compiler_params=_params("arbitrary", "arbitrary", "arbitrary"),
        name="attn_lat",
    )(q, k, v, kct, vct, bias)


def _na_bias_tables(rpb):
    c = np.arange(GRID_W)
    cs = np.clip(c - WIN_W // 2, 0, GRID_W - WIN_W)
    kc = np.arange(GRID_W)
    col_ok = (kc[None, :] >= cs[:, None]) & (kc[None, :] < cs[:, None] + WIN_W)
    dcol = kc[None, :] - c[:, None] + WIN_W - 1
    n_row, n_col = 2 * WIN_H - 1, 2 * WIN_W - 1
    band_rows = NA_BAND_LANES // GRID_W
    depth, heads = rpb.shape[0], rpb.shape[1]
    rpb2 = jnp.stack([jnp.pad(rpb[:, :, b:], ((0, 0), (0, 0), (0, band_rows - n_row + b), (0, 0)))
                      for b in range(LANES // GRID_W)], axis=2) * LOG2E
    onehot = (np.arange(n_col)[:, None, None] == dcol[None]).astype(np.float32)
    band = jnp.einsum('lhbid,dck->lhbcik', rpb2, onehot, precision=lax.Precision.HIGHEST)
    band = jnp.where(col_ok[None, None, None, :, None, :], band, MASK_VALUE)
    return band.reshape(depth, heads, LANES // GRID_W, GRID_W, NA_BAND_LANES)


def _merge_kernel(*refs, lat_tiles, split_x):
    x_refs, refs = refs[:1 + split_x], refs[1 + split_x:]
    (u_ref, h1_ref, yl_ref, yc_ref, s_ref, w3_ref, wgc_ref, wgn_ref, wgs_ref, wo_ref, mod_ref, g2_ref,
     x1_ref, u2_ref) = refs
    u = u_ref[...]
    branches = ((h1_ref[...], wgc_ref), (_pick(lat_tiles, yl_ref, yc_ref), wgn_ref), (s_ref[...], wgs_ref))
    y = None
    for c in range(D_MODEL // MXU_COLS):
        cs = slice(c * MXU_COLS, (c + 1) * MXU_COLS)
        mixed = None
        for b, (inp, wg_ref) in enumerate(branches):
            term = jax.nn.sigmoid(_dot(u, wg_ref[:, cs])) * _dot(inp, w3_ref[b, :, cs])
            mixed = term if mixed is None else mixed + term
        part = _dot(mixed.astype(BF16), wo_ref[cs, :])
        y = part if y is None else y + part
    m = mod_ref[0]
    x = _pick(lat_tiles, *x_refs) if split_x else x_refs[0][...]
    x1 = x + m[2:3] * y
    x1_ref[...] = x1
    u2_ref[...] = _norm_mod(x1, g2_ref[...], m[4:5], m[3:4]).astype(BF16)


def _merge(x_parts, u, h1, ya_lat, ya_ctx, s, w3, w_in, w_o, mods, g2, l, mod_row):
    n, d = u.shape
    tm = TM_MERGE
    lat_tiles = ya_lat.shape[0] // tm
    split_x = len(x_parts) == 2
    gate_off = (w_in.shape[2] - 3 * d) // d
    tile = pl.BlockSpec((tm, d), lambda i: (i, 0))
    split = _split_specs(tm, lat_tiles)
    gate_specs = [_resident((None, d, d), functools.partial(lambda b, i: (l, 0, gate_off + b), b)) for b in range(3)]
    return pl.pallas_call(
        functools.partial(_merge_kernel, lat_tiles=lat_tiles, split_x=split_x),
        grid=(n // tm,),
        in_specs=(split if split_x else [tile]) + [tile, tile] + split + [tile]
                 + [_resident((None, 3, d, d), lambda i: (l, 0, 0, 0))] + gate_specs
                 + [_resident((None, d, d), lambda i: (l, 0, 0)),
                    pl.BlockSpec((1, 6, d), lambda i: (mod_row(i, tm), 0, 0)),
                    pl.BlockSpec((1, d), lambda i: (0, 0))],
        out_specs=[tile, tile],
        out_shape=[jax.ShapeDtypeStruct((n, d), F32), jax.ShapeDtypeStruct((n, d), BF16)],
        compiler_params=_params("parallel"),
        name="merge",
    )(*x_parts, u, h1, ya_lat, ya_ctx, s, w3, w_in, w_in, w_in, w_o, mods, g2)


def _ffn_kernel(x_ref, u_ref, wg_ref, wu_ref, wd_ref, mod_ref, gn_ref, modn_ref, *rest, final):
    u = u_ref[...]
    f = None
    for c in range(wg_ref.shape[1] // MXU_COLS):
        cs = slice(c * MXU_COLS, (c + 1) * MXU_COLS)
        hidden = _silu(_dot(u, wg_ref[:, cs])) * _dot(u, wu_ref[:, cs])
        part = _dot(hidden.astype(BF16), wd_ref[cs, :])
        f = part if f is None else f + part
    x2 = x_ref[...] + mod_ref[0][5:6] * f
    if final:
        (y_ref,) = rest
        xn = x2 * lax.rsqrt(jnp.mean(x2 * x2, axis=-1, keepdims=True) + EPS)
        y_ref[...] = xn * gn_ref[...]
    else:
        x2_ref, un_ref = rest
        mn = modn_ref[0]
        x2_ref[...] = x2
        un_ref[...] = _norm_mod(x2, gn_ref[...], mn[1:2], mn[0:1]).astype(BF16)


def _ffn(x, u, wg, wu, wd, mods, g_next, mods_next, l, mod_row, final, row_off=0, rows=None):
    d = x.shape[1]
    n = x.shape[0] if rows is None else rows
    tm = TM_MERGE
    dff = wg.shape[2]
    assert dff % MXU_COLS == 0
    toff = row_off // tm
    tile = pl.BlockSpec((tm, d), lambda i: (i + toff, 0))
    out_tile = pl.BlockSpec((tm, d), lambda i: (i, 0))
    modspec = pl.BlockSpec((1, 6, d), lambda i: (mod_row(i + toff, tm), 0, 0))
    if final:
        out_shape = jax.ShapeDtypeStruct((n, d), F32)
        out_specs = out_tile
    else:
        out_shape = [jax.ShapeDtypeStruct((n, d), F32), jax.ShapeDtypeStruct((n, d), BF16)]
        out_specs = [out_tile, out_tile]
    return pl.pallas_call(
        functools.partial(_ffn_kernel, final=final),
        grid=(n // tm,),
        in_specs=[tile, tile,
                  _resident((None, d, dff), lambda i: (l, 0, 0)),
                  _resident((None, d, dff), lambda i: (l, 0, 0)),
                  _resident((None, dff, d), lambda i: (l, 0, 0)),
                  modspec,
                  pl.BlockSpec((1, d), lambda i: (0, 0)),
                  modspec],
        out_specs=out_specs,
        out_shape=out_shape,
        compiler_params=_params("parallel"),
        name="ffn_final" if final else "ffn",
    )(x, u, wg, wu, wd, mods, g_next, mods_next)


def kernel(x_prompt, x_sample, cache_k, cache_v, c, c_ctx, w_ada, b_ada, norm1_g, w_in, conv_dw_w, conv_dw_b, conv_ln_g, conv_ln_b, conv_pw_w, sc_dw_w, sc_out_w, na_rpb, na_out_w, w_o, norm2_g, ffn_w_gate, ffn_w_up, ffn_w_down, final_g):
    batch, seq, d = x_prompt.shape
    dec_batch, dec_seq, _ = x_sample.shape
    depth = w_ada.shape[0]
    n_lat = dec_batch * dec_seq
    n_ctx = batch * seq
    grid_rows = dec_seq // GRID_W
    assert d == D_MODEL and w_in.shape[2] == 11 * d
    assert seq == TM_CONV and dec_seq % TM_MM == 0 and n_ctx % TM_MM == 0
    assert grid_rows % NA_QROWS == 0 and grid_rows >= WIN_H and n_lat % TM_MERGE == 0

    def mod_row(i, tm):
        return jnp.minimum((i * tm) // dec_seq, dec_batch)

    n_cond = -(-(dec_batch + 1) // 8) * 8
    cond = jnp.zeros((n_cond, d), F32).at[:dec_batch].set(c).at[dec_batch].set(c_ctx)
    mods = _adaln(cond, w_ada, b_ada).reshape(depth, n_cond, 6, d)

    w_in_b = w_in.astype(BF16)
    w3_b = jnp.stack([conv_pw_w, na_out_w, sc_out_w], axis=1).astype(BF16)
    w_o_b = w_o.astype(BF16)
    wg_b, wu_b, wd_b = ffn_w_gate.astype(BF16), ffn_w_up.astype(BF16), ffn_w_down.astype(BF16)
    bias = _na_bias_tables(na_rpb)
    kct, vct = jnp.swapaxes(cache_k, 3, 4), jnp.swapaxes(cache_v, 3, 4)

    x_parts = (x_sample.reshape(n_lat, d), x_prompt.reshape(n_ctx, d))
    u = _prenorm(*x_parts, norm1_g[0][None], mods[0], mod_row)
    u_ctx = []
    for l in range(depth):
        u_ctx.append(u[n_lat:])
        h0, q, k, v, t, bg = _proj(u, w_in_b, l)
        h1, s = _conv_mixers(h0, t, bg, conv_dw_w[l], conv_dw_b[l][None], conv_ln_g[l][None],
                             conv_ln_b[l][None], sc_dw_w[l], n_lat, dec_seq)
        ya_lat = _attn_lat(q, k, v, kct, vct, bias, l, dec_batch, dec_seq)
        ya_ctx = _attn_ctx(q, k, v, n_lat, batch, seq)
        x, u2 = _merge(x_parts, u, h1, ya_lat, ya_ctx, s, w3_b, w_in_b, w_o_b, mods[l], norm2_g[l][None],
                       l, mod_row)
        if l + 1 < depth:
            x, u = _ffn(x, u2, wg_b, wu_b, wd_b, mods[l], norm1_g[l + 1][None], mods[l + 1], l, mod_row, False)
            x_parts = (x,)
        else:
            ffn_args = (x, u2, wg_b, wu_b, wd_b, mods[l], final_g[None], mods[l], l, mod_row, True)
            y_sample = _ffn(*ffn_args, row_off=0, rows=n_lat)
            y_prompt = _ffn(*ffn_args, row_off=n_lat, rows=n_ctx)
    new_kv = _proj_kv_ctx(jnp.stack(u_ctx, axis=0), w_in_b, batch, seq)

    return (y_prompt.reshape(batch, seq, d), y_sample.reshape(dec_batch, dec_seq, d), new_kv[0], new_kv[1])
```

```python
import functools

import numpy as np
import jax
import jax.numpy as jnp
from jax import lax
from jax.experimental import pallas as pl
from jax.experimental.pallas import tpu as pltpu

F32 = jnp.float32
BF16 = jnp.bfloat16

D_MODEL = 1024
HEADS = 16
HEAD_DIM = 64
GRID_W = 64
WIN_H = 8
WIN_W = 16
CONV_K = 31
SC_K = 3
EPS = 1e-6
LANES = 128
SUBLANES = 8
MXU_COLS = 256
HALO = 16
MASK_VALUE = -1e30
LOG2E = 1.4426950408889634

TM_MM = 1024
TM_MERGE = 512
TM_CONV = 256
TN_PROJ = 1024
NA_QROWS = 8
NA_HEADS = 4
CTX_LANES = 512
NA_BAND_LANES = 1024
VMEM_LIMIT = 48 * 1024 * 1024


def _dot(a, b):
    return jnp.dot(a, b, preferred_element_type=F32)


def _dot_nt(a, b):
    return lax.dot_general(a, b, (((1,), (1,)), ((), ())), preferred_element_type=F32)


def _silu(x):
    return x * jax.nn.sigmoid(x)


def _norm_mod(x, g, scale, shift):
    xn = x * lax.rsqrt(jnp.mean(x * x, axis=-1, keepdims=True) + EPS)
    return (xn * g) * (1.0 + scale) + shift


def _params(*sem):
    return pltpu.CompilerParams(dimension_semantics=sem, vmem_limit_bytes=VMEM_LIMIT)


def _resident(block_shape, index_map):
    return pl.BlockSpec(block_shape, index_map, pipeline_mode=pl.Buffered(1))


def _adaln_kernel(cond_ref, w_ref, b_ref, o_ref):
    a = _silu(cond_ref[...]).astype(BF16)
    o_ref[...] = _dot(a, w_ref[...].astype(BF16)) + b_ref[...]


def _adaln(cond, w_ada, b_ada):
    depth, d, n = w_ada.shape
    rows = cond.shape[0]
    tn = 1536
    return pl.pallas_call(
        _adaln_kernel,
        grid=(depth, n // tn),
        in_specs=[pl.BlockSpec((rows, d), lambda l, j: (0, 0)),
                  pl.BlockSpec((None, d, tn), lambda l, j: (l, 0, j)),
                  pl.BlockSpec((None, 1, tn), lambda l, j: (l, 0, j))],
        out_specs=pl.BlockSpec((None, rows, tn), lambda l, j: (l, 0, j)),
        out_shape=jax.ShapeDtypeStruct((depth, rows, n), F32),
        compiler_params=_params("parallel", "parallel"),
        name="adaln",
    )(cond, w_ada, b_ada.reshape(depth, 1, n))


def _split_specs(tm, lat_tiles):
    lat = pl.BlockSpec((tm, D_MODEL), lambda i: (jnp.minimum(i, lat_tiles - 1), 0))
    ctx = pl.BlockSpec((tm, D_MODEL), lambda i: (jnp.maximum(i - lat_tiles, 0), 0))
    return [lat, ctx]


def _pick(lat_tiles, lat_ref, ctx_ref):
    return jnp.where(pl.program_id(0) < lat_tiles, lat_ref[...], ctx_ref[...])


def _prenorm_kernel(xl_ref, xc_ref, g_ref, mod_ref, u_ref, *, lat_tiles):
    m = mod_ref[0]
    u_ref[...] = _norm_mod(_pick(lat_tiles, xl_ref, xc_ref), g_ref[...], m[1:2], m[0:1]).astype(BF16)


def _prenorm(x_lat, x_ctx, g, mods, mod_row):
    d = x_lat.shape[1]
    n = x_lat.shape[0] + x_ctx.shape[0]
    lat_tiles = x_lat.shape[0] // TM_MM
    return pl.pallas_call(
        functools.partial(_prenorm_kernel, lat_tiles=lat_tiles),
        grid=(n // TM_MM,),
        in_specs=_split_specs(TM_MM, lat_tiles)
                 + [pl.BlockSpec((1, d), lambda i: (0, 0)),
                    pl.BlockSpec((1, 6, d), lambda i: (mod_row(i, TM_MM), 0, 0))],
        out_specs=pl.BlockSpec((TM_MM, d), lambda i: (i, 0)),
        out_shape=jax.ShapeDtypeStruct((n, d), BF16),
        compiler_params=_params("parallel"),
        name="prenorm",
    )(x_lat, x_ctx, g, mods)


def _proj_kernel(u_ref, w_ref, h0_ref, q_ref, k_ref, v_ref, t_ref, bg_ref):
    u = u_ref[...]

    def seg(n, cs):
        return _dot(u, w_ref[:, n * D_MODEL + cs.start:n * D_MODEL + cs.stop])

    for c in range(D_MODEL // MXU_COLS):
        cs = slice(c * MXU_COLS, (c + 1) * MXU_COLS)
        h0_ref[:, cs] = (seg(0, cs) * jax.nn.sigmoid(seg(1, cs))).astype(BF16)
        q_ref[:, cs] = (seg(2, cs) * (HEAD_DIM ** -0.5 * LOG2E)).astype(BF16)
        k_ref[:, cs] = seg(3, cs).astype(BF16)
        v_ref[:, cs] = seg(4, cs).astype(BF16)
        bg_ref[:, cs] = seg(5, cs).astype(BF16)
        t_ref[:, cs] = (seg(6, cs) * seg(7, cs)).astype(BF16)


def _proj(u, w_in, l):
    n, d = u.shape
    tm = TM_MERGE
    tile = pl.BlockSpec((tm, d), lambda i: (i, 0))
    return pl.pallas_call(
        _proj_kernel,
        grid=(n // tm,),
        in_specs=[tile, _resident((None, d, 8 * d), lambda i: (l, 0, 0))],
        out_specs=[tile] * 6,
        out_shape=[jax.ShapeDtypeStruct((n, d), BF16)] * 6,
        compiler_params=_params("parallel"),
        name="proj",
    )(u, w_in)


def _proj_kv_ctx_kernel(u_ref, wk_ref, wv_ref, k_ref, v_ref):
    seq = k_ref.shape[2]
    u = u_ref[...]
    rk = _dot(u, wk_ref[...])
    rv = _dot(u, wv_ref[...])
    for b in range(k_ref.shape[0]):
        for h in range(TN_PROJ // HEAD_DIM):
            sl = slice(h * HEAD_DIM, (h + 1) * HEAD_DIM)
            k_ref[b, h] = rk[b * seq:(b + 1) * seq, sl]
            v_ref[b, h] = rv[b * seq:(b + 1) * seq, sl]


def _proj_kv_ctx(u_ctx, w_in, batch, seq):
    depth = u_ctx.shape[0]
    heads_per = TN_PROJ // HEAD_DIM
    koff, voff = 3 * D_MODEL // TN_PROJ, 4 * D_MODEL // TN_PROJ
    nb = TM_MERGE // seq
    out_spec = pl.BlockSpec((nb, None, heads_per, seq, HEAD_DIM), lambda l, j, b: (b, l, j, 0, 0))
    out_sds = jax.ShapeDtypeStruct((batch, depth, HEADS, seq, HEAD_DIM), F32)
    return pl.pallas_call(
        _proj_kv_ctx_kernel,
        grid=(depth, D_MODEL // TN_PROJ, batch // nb),
        in_specs=[pl.BlockSpec((None, nb * seq, D_MODEL), lambda l, j, b: (l, b, 0)),
                  pl.BlockSpec((None, D_MODEL, TN_PROJ), lambda l, j, b: (l, 0, j + koff)),
                  pl.BlockSpec((None, D_MODEL, TN_PROJ), lambda l, j, b: (l, 0, j + voff))],
        out_specs=[out_spec, out_spec],
        out_shape=[out_sds, out_sds],
        compiler_params=_params("parallel", "parallel", "arbitrary"),
        name="proj_kv_ctx",
    )(u_ctx, w_in, w_in)


def _fill_window(win_ref, prev_ref, cur_ref, next_ref, first, last, tm):
    prev = prev_ref[...].astype(F32)
    nxt = next_ref[...].astype(F32)
    win_ref[0:HALO, :] = jnp.where(first, 0.0, prev)
    win_ref[HALO:HALO + tm, :] = cur_ref[...].astype(F32)
    win_ref[HALO + tm:HALO + tm + HALO, :] = jnp.where(last, 0.0, nxt)


def _depthwise(win_ref, sh_ref, w_ref, out_ref, taps, tm):
    base = HALO - (taps - 1) // 2
    rows = 64
    sh_rows = sh_ref.shape[1]
    shifts = sorted({(base + k) % SUBLANES for k in range(taps)} - {0})

    def body(cb, carry):
        col = pl.multiple_of(cb * LANES, LANES)
        for s in shifts:
            sh_ref[s] = win_ref[pl.ds(s, sh_rows), pl.ds(col, LANES)]
        for r0 in range(0, tm, rows):
            acc = None
            for k in range(taps):
                off = base + r0 + k
                s, aligned = off % SUBLANES, off - off % SUBLANES
                if s == 0:
                    xk = win_ref[pl.ds(aligned, rows), pl.ds(col, LANES)]
                else:
                    xk = sh_ref[s, pl.ds(aligned, rows), :]
                term = xk * w_ref[pl.ds(k, 1), pl.ds(col, LANES)]
                acc = term if acc is None else acc + term
            out_ref[pl.ds(r0, rows), pl.ds(col, LANES)] = acc
        return carry

    lax.fori_loop(0, D_MODEL // LANES, body, 0)


def _conv_kernel(hp_ref, hc_ref, hn_ref, tp_ref, tc_ref, tn_ref, bg_ref,
                 wdw_ref, bdw_ref, lng_ref, lnb_ref, wsc_ref,
                 h1_ref, s_ref, win_ref, sh_ref, acc_ref, *, lat_tiles, lat_tiles_per_seq):
    tm = TM_CONV
    i = pl.program_id(0)
    is_lat = i < lat_tiles
    pos = i % lat_tiles_per_seq
    first = jnp.logical_or(jnp.logical_not(is_lat), pos == 0)
    last = jnp.logical_or(jnp.logical_not(is_lat), pos == lat_tiles_per_seq - 1)

    _fill_window(win_ref, hp_ref, hc_ref, hn_ref, first, last, tm)
    _depthwise(win_ref, sh_ref, wdw_ref, acc_ref, CONV_K, tm)
    y = acc_ref[...] + bdw_ref[...]
    mu = jnp.mean(y, axis=-1, keepdims=True)
    dlt = y - mu
    var = jnp.mean(dlt * dlt, axis=-1, keepdims=True)
    h = (dlt * lax.rsqrt(var + EPS)) * lng_ref[...] + lnb_ref[...]
    h1_ref[...] = _silu(h).astype(BF16)

    _fill_window(win_ref, tp_ref, tc_ref, tn_ref, first, last, tm)
    _depthwise(win_ref, sh_ref, wsc_ref, acc_ref, SC_K, tm)
    s_ref[...] = (bg_ref[...].astype(F32) * acc_ref[...]).astype(BF16)


def _conv_mixers(h0, t, bg, wdw, bdw, lng, lnb, wsc, lat_tokens, lat_seq):
    n, d = h0.shape
    tm = TM_CONV
    per = tm // HALO
    nblk = n // HALO
    cur = pl.BlockSpec((tm, d), lambda i: (i, 0))
    prev = pl.BlockSpec((HALO, d), lambda i: (jnp.maximum(i * per - 1, 0), 0))
    nxt = pl.BlockSpec((HALO, d), lambda i: (jnp.minimum((i + 1) * per, nblk - 1), 0))
    vec = pl.BlockSpec((1, d), lambda i: (0, 0))
    kern = functools.partial(_conv_kernel, lat_tiles=lat_tokens // tm, lat_tiles_per_seq=lat_seq // tm)
    return pl.pallas_call(
        kern,
        grid=(n // tm,),
        in_specs=[prev, cur, nxt, prev, cur, nxt, cur,
                  pl.BlockSpec((CONV_K, d), lambda i: (0, 0)), vec, vec, vec,
                  pl.BlockSpec((SC_K, d), lambda i: (0, 0))],
        out_specs=[cur, cur],
        out_shape=[jax.ShapeDtypeStruct((n, d), BF16)] * 2,
        scratch_shapes=[pltpu.VMEM((tm + 2 * HALO, d), F32),
                        pltpu.VMEM((SUBLANES, tm + 2 * HALO - SUBLANES, LANES), F32),
                        pltpu.VMEM((tm, d), F32)],
        compiler_params=_params("parallel"),
        name="conv_mixers",
    )(h0, h0, h0, t, t, t, bg, wdw, bdw, lng, lnb, wsc)


def _fold_lanes(op, *arrays):
    tiles = [a[:, i * LANES:(i + 1) * LANES] for a in arrays for i in range(a.shape[1] // LANES)]
    return functools.reduce(op, tiles)


def _attn_ctx_kernel(q_ref, k_ref, v_ref, o_ref):
    heads = [slice(h * HEAD_DIM, (h + 1) * HEAD_DIM) for h in range(CTX_LANES // HEAD_DIM)]
    scores = [_dot_nt(q_ref[:, sl], k_ref[:, sl]) for sl in heads]
    probs = []
    for s in scores:
        e = jnp.exp2(s - jnp.max(_fold_lanes(jnp.maximum, s), axis=-1, keepdims=True))
        probs.append((e.astype(BF16), jnp.sum(_fold_lanes(jnp.add, e), axis=-1, keepdims=True)))
    outs = [_dot(p, v_ref[:, sl]) / den for (p, den), sl in zip(probs, heads)]
    o_ref[...] = jnp.concatenate(outs, axis=-1).astype(BF16)


def _attn_ctx(q, k, v, lat_tokens, batch, seq):
    roff = lat_tokens // seq
    spec = pl.BlockSpec((seq, CTX_LANES), lambda b, hg: (b + roff, hg))
    return pl.pallas_call(
        _attn_ctx_kernel,
        grid=(batch, D_MODEL // CTX_LANES),
        in_specs=[spec, spec, spec],
        out_specs=pl.BlockSpec((seq, CTX_LANES), lambda b, hg: (b, hg)),
        out_shape=jax.ShapeDtypeStruct((batch * seq, D_MODEL), BF16),
        compiler_params=_params("parallel", "parallel"),
        name="attn_ctx",
    )(q, k, v)


def _attn_lat_kernel(q_ref, k_ref, v_ref, kct_ref, vct_ref, bias_ref, o_ref, ko_ref, vo_ref, *, grid_rows):
    rb = pl.program_id(2)
    nloc = WIN_H * GRID_W
    per_tile = LANES // HEAD_DIM

    @pl.when(rb == 0)
    def _():
        for p in range(NA_HEADS // per_tile):
            ko_ref[p] = k_ref[:, p * LANES + HEAD_DIM:(p + 1) * LANES]
            vo_ref[p] = v_ref[:, p * LANES + HEAD_DIM:(p + 1) * LANES]

    def window(ref, odd_ref, h, start):
        if h % per_tile == 0:
            return ref[pl.ds(start, nloc), h * HEAD_DIM:(h + 1) * HEAD_DIM]
        return odd_ref[h // per_tile, pl.ds(start, nloc), :]

    starts, band_sel, band_off = [], [], []
    for j in range(NA_QROWS):
        r = rb * NA_QROWS + j
        rs = jnp.clip(r - WIN_H // 2, 0, grid_rows - WIN_H)
        starts.append(pl.multiple_of(rs * GRID_W, GRID_W))
        lane0 = (WIN_H - 1 - (r - rs)) * GRID_W
        band_sel.append((lane0 // GRID_W) % (LANES // GRID_W))
        band_off.append(pl.multiple_of((lane0 // LANES) * LANES, LANES))

    scores = []
    for h in range(NA_HEADS):
        q = q_ref[:, h * HEAD_DIM:(h + 1) * HEAD_DIM]
        s_ctx = _dot(q, kct_ref[h].astype(BF16))
        s_rows = []
        for j in range(NA_QROWS):
            bias = bias_ref[h, band_sel[j], :, pl.ds(band_off[j], nloc)]
            s_rows.append(_dot_nt(q[j * GRID_W:(j + 1) * GRID_W], window(k_ref, ko_ref, h, starts[j])) + bias)
        scores.append((jnp.concatenate(s_rows, axis=0), s_ctx))
    probs = []
    for s_loc, s_ctx in scores:
        m = jnp.max(_fold_lanes(jnp.maximum, s_loc, s_ctx), axis=-1, keepdims=True)
        e_loc = jnp.exp2(s_loc - m)
        e_ctx = jnp.exp2(s_ctx - m)
        den = jnp.sum(_fold_lanes(jnp.add, e_loc, e_ctx), axis=-1, keepdims=True)
        probs.append((e_loc.astype(BF16), e_ctx.astype(BF16), den))
    outs = []
    for h in range(NA_HEADS):
        p_loc, p_ctx, den = probs[h]
        o_rows = [_dot(p_loc[j * GRID_W:(j + 1) * GRID_W], window(v_ref, vo_ref, h, starts[j]))
                  for j in range(NA_QROWS)]
        o = jnp.concatenate(o_rows, axis=0) + _dot_nt(p_ctx, vct_ref[h].astype(BF16))
        outs.append(o / den)
    o_ref[...] = jnp.concatenate(outs, axis=-1).astype(BF16)


def _attn_lat(q, k, v, kct, vct, bias, l, dec_batch, dec_seq):
    grid_rows = dec_seq // GRID_W
    nrb = grid_rows // NA_QROWS
    tq = NA_QROWS * GRID_W
    width = NA_HEADS * HEAD_DIM
    past = kct.shape[4]
    seq_spec = pl.BlockSpec((dec_seq, width), lambda hg, b, rb: (b, hg))
    ctx_spec = pl.BlockSpec((None, None, NA_HEADS, HEAD_DIM, past), lambda hg, b, rb: (b, l, hg, 0, 0))
    q_spec = pl.BlockSpec((tq, width), lambda hg, b, rb: (b * nrb + rb, hg))
    return pl.pallas_call(
        functools.partial(_attn_lat_kernel, grid_rows=grid_rows),
        grid=(D_MODEL // width, dec_batch, nrb),
        in_specs=[q_spec, seq_spec, seq_spec, ctx_spec, ctx_spec,
                  pl.BlockSpec((None, NA_HEADS, LANES // GRID_W, GRID_W, NA_BAND_LANES),
                               lambda hg, b, rb: (l, hg, 0, 0, 0))],
        out_specs=q_spec,
        out_shape=jax.ShapeDtypeStruct((dec_batch * dec_seq, D_MODEL), BF16),
        scratch_shapes=[pltpu.VMEM((NA_HEADS * HEAD_DIM // LANES, dec_seq, HEAD_DIM), BF16)] * 2,
        compiler_params=_params("arbitrary", "arbitrary", "arbitrary"),
        name="attn_lat",
    )(q, k, v, kct, vct, bias)


def _na_bias_tables(rpb):
    c = np.arange(GRID_W)
    cs = np.clip(c - WIN_W // 2, 0, GRID_W - WIN_W)
    kc = np.arange(GRID_W)
    col_ok = (kc[None, :] >= cs[:, None]) & (kc[None, :] < cs[:, None] + WIN_W)
    dcol = kc[None, :] - c[:, None] + WIN_W - 1
    n_row, n_col = 2 * WIN_H - 1, 2 * WIN_W - 1
    band_rows = NA_BAND_LANES // GRID_W
    depth, heads = rpb.shape[0], rpb.shape[1]
    rpb2 = jnp.stack([jnp.pad(rpb[:, :, b:], ((0, 0), (0, 0), (0, band_rows - n_row + b), (0, 0)))
                      for b in range(LANES // GRID_W)], axis=2) * LOG2E
    rpb2 = jnp.concatenate([rpb2, jnp.full(rpb2.shape[:-1] + (1,), MASK_VALUE, F32)], axis=-1)
    onehot = np.concatenate([(np.arange(n_col)[:, None, None] == dcol[None]) & col_ok[None], ~col_ok[None]],
                            axis=0).astype(np.float32)
    band = jnp.einsum('lhbid,dck->lhbcik', rpb2, onehot, precision=lax.Precision.HIGHEST)
    return band.reshape(depth, heads, LANES // GRID_W, GRID_W, NA_BAND_LANES)


def _merge_kernel(*refs, lat_tiles, split_x):
    x_refs, refs = refs[:1 + split_x], refs[1 + split_x:]
    (u_ref, h1_ref, yl_ref, yc_ref, s_ref, w3_ref, wgc_ref, wgn_ref, wgs_ref, wo_ref, mod_ref, g2_ref,
     x1_ref, u2_ref) = refs
    u = u_ref[...]
    branches = ((h1_ref[...], wgc_ref), (_pick(lat_tiles, yl_ref, yc_ref), wgn_ref), (s_ref[...], wgs_ref))
    y = None
    for c in range(D_MODEL // MXU_COLS):
        cs = slice(c * MXU_COLS, (c + 1) * MXU_COLS)
        mixed = None
        for b, (inp, wg_ref) in enumerate(branches):
            term = jax.nn.sigmoid(_dot(u, wg_ref[:, cs])) * _dot(inp, w3_ref[b, :, cs])
            mixed = term if mixed is None else mixed + term
        part = _dot(mixed.astype(BF16), wo_ref[cs, :])
        y = part if y is None else y + part
    m = mod_ref[0]
    x = _pick(lat_tiles, *x_refs) if split_x else x_refs[0][...]
    x1 = x + m[2:3] * y
    x1_ref[...] = x1
    u2_ref[...] = _norm_mod(x1, g2_ref[...], m[4:5], m[3:4]).astype(BF16)


def _merge(x_parts, u, h1, ya_lat, ya_ctx, s, w3, w_in, w_o, mods, g2, l, mod_row):
    n, d = u.shape
    tm = TM_MERGE
    lat_tiles = ya_lat.shape[0] // tm
    split_x = len(x_parts) == 2
    gate_off = (w_in.shape[2] - 3 * d) // d
    tile = pl.BlockSpec((tm, d), lambda i: (i, 0))
    split = _split_specs(tm, lat_tiles)
    gate_specs = [_resident((None, d, d), functools.partial(lambda b, i: (l, 0, gate_off + b), b)) for b in range(3)]
    return pl.pallas_call(
        functools.partial(_merge_kernel, lat_tiles=lat_tiles, split_x=split_x),
        grid=(n // tm,),
        in_specs=(split if split_x else [tile]) + [tile, tile] + split + [tile]
                 + [_resident((None, 3, d, d), lambda i: (l, 0, 0, 0))] + gate_specs
                 + [_resident((None, d, d), lambda i: (l, 0, 0)),
                    pl.BlockSpec((1, 6, d), lambda i: (mod_row(i, tm), 0, 0)),
                    pl.BlockSpec((1, d), lambda i: (0, 0))],
        out_specs=[tile, tile],
        out_shape=[jax.ShapeDtypeStruct((n, d), F32), jax.ShapeDtypeStruct((n, d), BF16)],
        compiler_params=_params("parallel"),
        name="merge",
    )(*x_parts, u, h1, ya_lat, ya_ctx, s, w3, w_in, w_in, w_in, w_o, mods, g2)


def _ffn_kernel(x_ref, u_ref, wg_ref, wu_ref, wd_ref, mod_ref, gn_ref, modn_ref, *rest, final):
    u = u_ref[...]
    f = None
    for c in range(wg_ref.shape[1] // MXU_COLS):
        cs = slice(c * MXU_COLS, (c + 1) * MXU_COLS)
        hidden = _silu(_dot(u, wg_ref[:, cs])) * _dot(u, wu_ref[:, cs])
        part = _dot(hidden.astype(BF16), wd_ref[cs, :])
        f = part if f is None else f + part
    x2 = x_ref[...] + mod_ref[0][5:6] * f
    if final:
        (y_ref,) = rest
        xn = x2 * lax.rsqrt(jnp.mean(x2 * x2, axis=-1, keepdims=True) + EPS)
        y_ref[...] = xn * gn_ref[...]
    else:
        x2_ref, un_ref = rest
        mn = modn_ref[0]
        x2_ref[...] = x2
        un_ref[...] = _norm_mod(x2, gn_ref[...], mn[1:2], mn[0:1]).astype(BF16)


def _ffn(x, u, wg, wu, wd, mods, g_next, mods_next, l, mod_row, final, row_off=0, rows=None):
    d = x.shape[1]
    n = x.shape[0] if rows is None else rows
    tm = TM_MERGE
    dff = wg.shape[2]
    assert dff % MXU_COLS == 0
    toff = row_off // tm
    tile = pl.BlockSpec((tm, d), lambda i: (i + toff, 0))
    out_tile = pl.BlockSpec((tm, d), lambda i: (i, 0))
    modspec = pl.BlockSpec((1, 6, d), lambda i: (mod_row(i + toff, tm), 0, 0))
    if final:
        out_shape = jax.ShapeDtypeStruct((n, d), F32)
        out_specs = out_tile
    else:
        out_shape = [jax.ShapeDtypeStruct((n, d), F32), jax.ShapeDtypeStruct((n, d), BF16)]
        out_specs = [out_tile, out_tile]
    return pl.pallas_call(
        functools.partial(_ffn_kernel, final=final),
        grid=(n // tm,),
        in_specs=[tile, tile,
                  _resident((None, d, dff), lambda i: (l, 0, 0)),
                  _resident((None, d, dff), lambda i: (l, 0, 0)),
                  _resident((None, dff, d), lambda i: (l, 0, 0)),
                  modspec,
                  pl.BlockSpec((1, d), lambda i: (0, 0)),
                  modspec],
        out_specs=out_specs,
        out_shape=out_shape,
        compiler_params=_params("parallel"),
        name="ffn_final" if final else "ffn",
    )(x, u, wg, wu, wd, mods, g_next, mods_next)


def kernel(x_prompt, x_sample, cache_k, cache_v, c, c_ctx, w_ada, b_ada, norm1_g, w_in, conv_dw_w, conv_dw_b, conv_ln_g, conv_ln_b, conv_pw_w, sc_dw_w, sc_out_w, na_rpb, na_out_w, w_o, norm2_g, ffn_w_gate, ffn_w_up, ffn_w_down, final_g):
    batch, seq, d = x_prompt.shape
    dec_batch, dec_seq, _ = x_sample.shape
    depth = w_ada.shape[0]
    n_lat = dec_batch * dec_seq
    n_ctx = batch * seq
    grid_rows = dec_seq // GRID_W
    assert d == D_MODEL and w_in.shape[2] == 11 * d
    assert seq == TM_CONV and dec_seq % TM_MM == 0 and n_ctx % TM_MM == 0
    assert grid_rows % NA_QROWS == 0 and grid_rows >= WIN_H and n_lat % TM_MERGE == 0

    def mod_row(i, tm):
        return jnp.minimum((i * tm) // dec_seq, dec_batch)

    n_cond = -(-(dec_batch + 1) // 8) * 8
    cond = jnp.zeros((n_cond, d), F32).at[:dec_batch].set(c).at[dec_batch].set(c_ctx)
    mods = _adaln(cond, w_ada, b_ada).reshape(depth, n_cond, 6, d)

    w_in_b = w_in.astype(BF16)
    w3_b = jnp.stack([conv_pw_w, na_out_w, sc_out_w], axis=1).astype(BF16)
    w_o_b = w_o.astype(BF16)
    wg_b, wu_b, wd_b = ffn_w_gate.astype(BF16), ffn_w_up.astype(BF16), ffn_w_down.astype(BF16)
    bias = _na_bias_tables(na_rpb)
    kct, vct = jnp.swapaxes(cache_k, 3, 4), jnp.swapaxes(cache_v, 3, 4)

    x_parts = (x_sample.reshape(n_lat, d), x_prompt.reshape(n_ctx, d))
    u = _prenorm(*x_parts, norm1_g[0][None], mods[0], mod_row)
    u_ctx = []
    for l in range(depth):
        u_ctx.append(u[n_lat:])
        h0, q, k, v, t, bg = _proj(u, w_in_b, l)
        h1, s = _conv_mixers(h0, t, bg, conv_dw_w[l], conv_dw_b[l][None], conv_ln_g[l][None],
                             conv_ln_b[l][None], sc_dw_w[l], n_lat, dec_seq)
        ya_lat = _attn_lat(q, k, v, kct, vct, bias, l, dec_batch, dec_seq)
        ya_ctx = _attn_ctx(q, k, v, n_lat, batch, seq)
        x, u2 = _merge(x_parts, u, h1, ya_lat, ya_ctx, s, w3_b, w_in_b, w_o_b, mods[l], norm2_g[l][None],
                       l, mod_row)
        if l + 1 < depth:
            x, u = _ffn(x, u2, wg_b, wu_b, wd_b, mods[l], norm1_g[l + 1][None], mods[l + 1], l, mod_row, False)
            x_parts = (x,)
        else:
            ffn_args = (x, u2, wg_b, wu_b, wd_b, mods[l], final_g[None], mods[l], l, mod_row, True)
            y_sample = _ffn(*ffn_args, row_off=0, rows=n_lat)
            y_prompt = _ffn(*ffn_args, row_off=n_lat, rows=n_ctx)
    new_kv = _proj_kv_ctx(jnp.stack(u_ctx, axis=0), w_in_b, batch, seq)

    return (y_prompt.reshape(batch, seq, d), y_sample.reshape(dec_batch, dec_seq, d), new_kv[0], new_kv[1])
```

```python
import functools

import numpy as np
import jax
import jax.numpy as jnp
from jax import lax
from jax.experimental import pallas as pl
from jax.experimental.pallas import tpu as pltpu

F32 = jnp.float32
BF16 = jnp.bfloat16

D_MODEL = 1024
HEADS = 16
HEAD_DIM = 64
GRID_W = 64
WIN_H = 8
WIN_W = 16
CONV_K = 31
SC_K = 3
EPS = 1e-6
LANES = 128
SUBLANES = 8
MXU_COLS = 256
HALO = 16
MASK_VALUE = -1e30
LOG2E = 1.4426950408889634

TM_MM = 1024
TM_MERGE = 512
TM_CONV = 256
TN_PROJ = 1024
NA_QROWS = 8
NA_HEADS = 4
CTX_LANES = 512
NA_BAND_LANES = 1024
VMEM_LIMIT = 48 * 1024 * 1024


def _dot(a, b):
    return jnp.dot(a, b, preferred_element_type=F32)


def _dot_nt(a, b):
    return lax.dot_general(a, b, (((1,), (1,)), ((), ())), preferred_element_type=F32)


def _silu(x):
    return x * jax.nn.sigmoid(x)


def _norm_mod(x, g, scale, shift):
    xn = x * lax.rsqrt(jnp.mean(x * x, axis=-1, keepdims=True) + EPS)
    return (xn * g) * (1.0 + scale) + shift


def _params(*sem):
    return pltpu.CompilerParams(dimension_semantics=sem, vmem_limit_bytes=VMEM_LIMIT)


def _resident(block_shape, index_map):
    return pl.BlockSpec(block_shape, index_map, pipeline_mode=pl.Buffered(1))


def _adaln_kernel(cond_ref, w_ref, b_ref, o_ref):
    a = _silu(cond_ref[...]).astype(BF16)
    o_ref[...] = _dot(a, w_ref[...].astype(BF16)) + b_ref[...]


def _adaln(cond, w_ada, b_ada):
    depth, d, n = w_ada.shape
    rows = cond.shape[0]
    tn = 1536
    return pl.pallas_call(
        _adaln_kernel,
        grid=(depth, n // tn),
        in_specs=[pl.BlockSpec((rows, d), lambda l, j: (0, 0)),
                  pl.BlockSpec((None, d, tn), lambda l, j: (l, 0, j)),
                  pl.BlockSpec((None, 1, tn), lambda l, j: (l, 0, j))],
        out_specs=pl.BlockSpec((None, rows, tn), lambda l, j: (l, 0, j)),
        out_shape=jax.ShapeDtypeStruct((depth, rows, n), F32),
        compiler_params=_params("parallel", "parallel"),
        name="adaln",
    )(cond, w_ada, b_ada.reshape(depth, 1, n))


def _split_specs(tm, lat_tiles):
    lat = pl.BlockSpec((tm, D_MODEL), lambda i: (jnp.minimum(i, lat_tiles - 1), 0))
    ctx = pl.BlockSpec((tm, D_MODEL), lambda i: (jnp.maximum(i - lat_tiles, 0), 0))
    return [lat, ctx]


def _pick(lat_tiles, lat_ref, ctx_ref):
    return jnp.where(pl.program_id(0) < lat_tiles, lat_ref[...], ctx_ref[...])


def _prenorm_kernel(xl_ref, xc_ref, g_ref, mod_ref, u_ref, *, lat_tiles):
    m = mod_ref[0]
    u_ref[...] = _norm_mod(_pick(lat_tiles, xl_ref, xc_ref), g_ref[...], m[1:2], m[0:1]).astype(BF16)


def _prenorm(x_lat, x_ctx, g, mods, mod_row):
    d = x_lat.shape[1]
    n = x_lat.shape[0] + x_ctx.shape[0]
    lat_tiles = x_lat.shape[0] // TM_MM
    return pl.pallas_call(
        functools.partial(_prenorm_kernel, lat_tiles=lat_tiles),
        grid=(n // TM_MM,),
        in_specs=_split_specs(TM_MM, lat_tiles)
                 + [pl.BlockSpec((1, d), lambda i: (0, 0)),
                    pl.BlockSpec((1, 6, d), lambda i: (mod_row(i, TM_MM), 0, 0))],
        out_specs=pl.BlockSpec((TM_MM, d), lambda i: (i, 0)),
        out_shape=jax.ShapeDtypeStruct((n, d), BF16),
        compiler_params=_params("parallel"),
        name="prenorm",
    )(x_lat, x_ctx, g, mods)


def _proj_kernel(u_ref, w_ref, h0_ref, q_ref, k_ref, v_ref, t_ref, bg_ref):
    u = u_ref[...]

    def seg(n, cs):
        return _dot(u, w_ref[:, n * D_MODEL + cs.start:n * D_MODEL + cs.stop])

    for c in range(D_MODEL // MXU_COLS):
        cs = slice(c * MXU_COLS, (c + 1) * MXU_COLS)
        h0_ref[:, cs] = (seg(0, cs) * jax.nn.sigmoid(seg(1, cs))).astype(BF16)
        q_ref[:, cs] = (seg(2, cs) * (HEAD_DIM ** -0.5 * LOG2E)).astype(BF16)
        k_ref[:, cs] = seg(3, cs).astype(BF16)
        v_ref[:, cs] = seg(4, cs).astype(BF16)
        bg_ref[:, cs] = seg(5, cs).astype(BF16)
        t_ref[:, cs] = (seg(6, cs) * seg(7, cs)).astype(BF16)


def _proj(u, w_in, l):
    n, d = u.shape
    tm = TM_MERGE
    tile = pl.BlockSpec((tm, d), lambda i: (i, 0))
    return pl.pallas_call(
        _proj_kernel,
        grid=(n // tm,),
        in_specs=[tile, _resident((None, d, 8 * d), lambda i: (l, 0, 0))],
        out_specs=[tile] * 6,
        out_shape=[jax.ShapeDtypeStruct((n, d), BF16)] * 6,
        compiler_params=_params("parallel"),
        name="proj",
    )(u, w_in)


def _proj_kv_ctx_kernel(*refs):
    u_refs, (wk_ref, wv_ref, k_ref, v_ref) = refs[:-4], refs[-4:]
    seq = k_ref.shape[2]
    u = u_refs[0][...]
    for layer in range(1, len(u_refs)):
        u = jnp.where(pl.program_id(0) == layer, u_refs[layer][...], u)
    rk = _dot(u, wk_ref[...])
    rv = _dot(u, wv_ref[...])
    for b in range(k_ref.shape[0]):
        for h in range(TN_PROJ // HEAD_DIM):
            sl = slice(h * HEAD_DIM, (h + 1) * HEAD_DIM)
            k_ref[b, h] = rk[b * seq:(b + 1) * seq, sl]
            v_ref[b, h] = rv[b * seq:(b + 1) * seq, sl]


def _proj_kv_ctx(us, w_in, n_lat, batch, seq):
    depth = len(us)
    heads_per = TN_PROJ // HEAD_DIM
    koff, voff = 3 * D_MODEL // TN_PROJ, 4 * D_MODEL // TN_PROJ
    nb = TM_MERGE // seq
    roff = n_lat // (nb * seq)
    out_spec = pl.BlockSpec((nb, None, heads_per, seq, HEAD_DIM), lambda l, j, b: (b, l, j, 0, 0))
    out_sds = jax.ShapeDtypeStruct((batch, depth, HEADS, seq, HEAD_DIM), F32)
    u_specs = [pl.BlockSpec((nb * seq, D_MODEL),
                            functools.partial(lambda a, l, j, b: (roff + jnp.where(l == a, b, 0), 0), a))
               for a in range(depth)]
    return pl.pallas_call(
        _proj_kv_ctx_kernel,
        grid=(depth, D_MODEL // TN_PROJ, batch // nb),
        in_specs=u_specs
                 + [pl.BlockSpec((None, D_MODEL, TN_PROJ), lambda l, j, b: (l, 0, j + koff)),
                    pl.BlockSpec((None, D_MODEL, TN_PROJ), lambda l, j, b: (l, 0, j + voff))],
        out_specs=[out_spec, out_spec],
        out_shape=[out_sds, out_sds],
        compiler_params=_params("parallel", "parallel", "arbitrary"),
        name="proj_kv_ctx",
    )(*us, w_in, w_in)


def _fill_window(win_ref, prev_ref, cur_ref, next_ref, first, last, tm):
    prev = prev_ref[...].astype(F32)
    nxt = next_ref[...].astype(F32)
    win_ref[0:HALO, :] = jnp.where(first, 0.0, prev)
    win_ref[HALO:HALO + tm, :] = cur_ref[...].astype(F32)
    win_ref[HALO + tm:HALO + tm + HALO, :] = jnp.where(last, 0.0, nxt)


def _depthwise(win_ref, sh_ref, w_ref, out_ref, taps, tm):
    base = HALO - (taps - 1) // 2
    rows = 64
    sh_rows = sh_ref.shape[1]
    shifts = sorted({(base + k) % SUBLANES for k in range(taps)} - {0})

    def body(cb, carry):
        col = pl.multiple_of(cb * LANES, LANES)
        for s in shifts:
            sh_ref[s] = win_ref[pl.ds(s, sh_rows), pl.ds(col, LANES)]
        for r0 in range(0, tm, rows):
            acc = None
            for k in range(taps):
                off = base + r0 + k
                s, aligned = off % SUBLANES, off - off % SUBLANES
                if s == 0:
                    xk = win_ref[pl.ds(aligned, rows), pl.ds(col, LANES)]
                else:
                    xk = sh_ref[s, pl.ds(aligned, rows), :]
                term = xk * w_ref[pl.ds(k, 1), pl.ds(col, LANES)]
                acc = term if acc is None else acc + term
            out_ref[pl.ds(r0, rows), pl.ds(col, LANES)] = acc
        return carry

    lax.fori_loop(0, D_MODEL // LANES, body, 0)


def _conv_kernel(hp_ref, hc_ref, hn_ref, tp_ref, tc_ref, tn_ref, bg_ref,
                 wdw_ref, bdw_ref, lng_ref, lnb_ref, wsc_ref,
                 h1_ref, s_ref, win_ref, sh_ref, acc_ref, *, lat_tiles, lat_tiles_per_seq):
    tm = TM_CONV
    i = pl.program_id(0)
    is_lat = i < lat_tiles
    pos = i % lat_tiles_per_seq
    first = jnp.logical_or(jnp.logical_not(is_lat), pos == 0)
    last = jnp.logical_or(jnp.logical_not(is_lat), pos == lat_tiles_per_seq - 1)

    _fill_window(win_ref, hp_ref, hc_ref, hn_ref, first, last, tm)
    _depthwise(win_ref, sh_ref, wdw_ref, acc_ref, CONV_K, tm)
    y = acc_ref[...] + bdw_ref[...]
    mu = jnp.mean(y, axis=-1, keepdims=True)
    dlt = y - mu
    var = jnp.mean(dlt * dlt, axis=-1, keepdims=True)
    h = (dlt * lax.rsqrt(var + EPS)) * lng_ref[...] + lnb_ref[...]
    h1_ref[...] = _silu(h).astype(BF16)

    _fill_window(win_ref, tp_ref, tc_ref, tn_ref, first, last, tm)
    _depthwise(win_ref, sh_ref, wsc_ref, acc_ref, SC_K, tm)
    s_ref[...] = (bg_ref[...].astype(F32) * acc_ref[...]).astype(BF16)


def _conv_mixers(h0, t, bg, wdw, bdw, lng, lnb, wsc, lat_tokens, lat_seq):
    n, d = h0.shape
    tm = TM_CONV
    per = tm // HALO
    nblk = n // HALO
    cur = pl.BlockSpec((tm, d), lambda i: (i, 0))
    prev = pl.BlockSpec((HALO, d), lambda i: (jnp.maximum(i * per - 1, 0), 0))
    nxt = pl.BlockSpec((HALO, d), lambda i: (jnp.minimum((i + 1) * per, nblk - 1), 0))
    vec = pl.BlockSpec((1, d), lambda i: (0, 0))
    kern = functools.partial(_conv_kernel, lat_tiles=lat_tokens // tm, lat_tiles_per_seq=lat_seq // tm)
    return pl.pallas_call(
        kern,
        grid=(n // tm,),
        in_specs=[prev, cur, nxt, prev, cur, nxt, cur,
                  pl.BlockSpec((CONV_K, d), lambda i: (0, 0)), vec, vec, vec,
                  pl.BlockSpec((SC_K, d), lambda i: (0, 0))],
        out_specs=[cur, cur],
        out_shape=[jax.ShapeDtypeStruct((n, d), BF16)] * 2,
        scratch_shapes=[pltpu.VMEM((tm + 2 * HALO, d), F32),
                        pltpu.VMEM((SUBLANES, tm + 2 * HALO - SUBLANES, LANES), F32),
                        pltpu.VMEM((tm, d), F32)],
        compiler_params=_params("parallel"),
        name="conv_mixers",
    )(h0, h0, h0, t, t, t, bg, wdw, bdw, lng, lnb, wsc)


def _fold_lanes(op, *arrays):
    tiles = [a[:, i * LANES:(i + 1) * LANES] for a in arrays for i in range(a.shape[1] // LANES)]
    return functools.reduce(op, tiles)


def _attn_ctx_kernel(q_ref, k_ref, v_ref, o_ref):
    heads = [slice(h * HEAD_DIM, (h + 1) * HEAD_DIM) for h in range(CTX_LANES // HEAD_DIM)]
    scores = [_dot_nt(q_ref[:, sl], k_ref[:, sl]) for sl in heads]
    probs = []
    for s in scores:
        e = jnp.exp2(s - jnp.max(_fold_lanes(jnp.maximum, s), axis=-1, keepdims=True))
        probs.append((e.astype(BF16), jnp.sum(_fold_lanes(jnp.add, e), axis=-1, keepdims=True)))
    outs = [_dot(p, v_ref[:, sl]) / den for (p, den), sl in zip(probs, heads)]
    o_ref[...] = jnp.concatenate(outs, axis=-1).astype(BF16)


def _attn_ctx(q, k, v, lat_tokens, batch, seq):
    roff = lat_tokens // seq
    spec = pl.BlockSpec((seq, CTX_LANES), lambda b, hg: (b + roff, hg))
    return pl.pallas_call(
        _attn_ctx_kernel,
        grid=(batch, D_MODEL // CTX_LANES),
        in_specs=[spec, spec, spec],
        out_specs=pl.BlockSpec((seq, CTX_LANES), lambda b, hg: (b, hg)),
        out_shape=jax.ShapeDtypeStruct((batch * seq, D_MODEL), BF16),
        compiler_params=_params("parallel", "parallel"),
        name="attn_ctx",
    )(q, k, v)


def _attn_lat_kernel(q_ref, k_ref, v_ref, kct_ref, vct_ref, bias_ref, o_ref, ko_ref, vo_ref, *, grid_rows):
    rb = pl.program_id(2)
    nloc = WIN_H * GRID_W
    per_tile = LANES // HEAD_DIM

    @pl.when(rb == 0)
    def _():
        for p in range(NA_HEADS // per_tile):
            ko_ref[p] = k_ref[:, p * LANES + HEAD_DIM:(p + 1) * LANES]
            vo_ref[p] = v_ref[:, p * LANES + HEAD_DIM:(p + 1) * LANES]

    def window(ref, odd_ref, h, start):
        if h % per_tile == 0:
            return ref[pl.ds(start, nloc), h * HEAD_DIM:(h + 1) * HEAD_DIM]
        return odd_ref[h // per_tile, pl.ds(start, nloc), :]

    starts, band_sel, band_off = [], [], []
    for j in range(NA_QROWS):
        r = rb * NA_QROWS + j
        rs = jnp.clip(r - WIN_H // 2, 0, grid_rows - WIN_H)
        starts.append(pl.multiple_of(rs * GRID_W, GRID_W))
        lane0 = (WIN_H - 1 - (r - rs)) * GRID_W
        band_sel.append((lane0 // GRID_W) % (LANES // GRID_W))
        band_off.append(pl.multiple_of((lane0 // LANES) * LANES, LANES))

    scores = []
    for h in range(NA_HEADS):
        q = q_ref[:, h * HEAD_DIM:(h + 1) * HEAD_DIM]
        s_ctx = _dot(q, kct_ref[h].astype(BF16))
        s_rows = []
        for j in range(NA_QROWS):
            bias = bias_ref[h, band_sel[j], :, pl.ds(band_off[j], nloc)]
            s_rows.append(_dot_nt(q[j * GRID_W:(j + 1) * GRID_W], window(k_ref, ko_ref, h, starts[j])) + bias)
        scores.append((jnp.concatenate(s_rows, axis=0), s_ctx))
    probs = []
    for s_loc, s_ctx in scores:
        m = jnp.max(_fold_lanes(jnp.maximum, s_loc, s_ctx), axis=-1, keepdims=True)
        e_loc = jnp.exp2(s_loc - m)
        e_ctx = jnp.exp2(s_ctx - m)
        den = jnp.sum(_fold_lanes(jnp.add, e_loc, e_ctx), axis=-1, keepdims=True)
        probs.append((e_loc.astype(BF16), e_ctx.astype(BF16), den))
    outs = []
    for h in range(NA_HEADS):
        p_loc, p_ctx, den = probs[h]
        o_rows = [_dot(p_loc[j * GRID_W:(j + 1) * GRID_W], window(v_ref, vo_ref, h, starts[j]))
                  for j in range(NA_QROWS)]
        o = jnp.concatenate(o_rows, axis=0) + _dot_nt(p_ctx, vct_ref[h].astype(BF16))
        outs.append(o / den)
    o_ref[...] = jnp.concatenate(outs, axis=-1).astype(BF16)


def _attn_lat(q, k, v, kct, vct, bias, l, dec_batch, dec_seq):
    grid_rows = dec_seq // GRID_W
    nrb = grid_rows // NA_QROWS
    tq = NA_QROWS * GRID_W
    width = NA_HEADS * HEAD_DIM
    past = kct.shape[4]
    seq_spec = pl.BlockSpec((dec_seq, width), lambda hg, b, rb: (b, hg))
    ctx_spec = pl.BlockSpec((None, None, NA_HEADS, HEAD_DIM, past), lambda hg, b, rb: (b, l, hg, 0, 0))
    q_spec = pl.BlockSpec((tq, width), lambda hg, b, rb: (b * nrb + rb, hg))
    return pl.pallas_call(
        functools.partial(_attn_lat_kernel, grid_rows=grid_rows),
        grid=(D_MODEL // width, dec_batch, nrb),
        in_specs=[q_spec, seq_spec, seq_spec, ctx_spec, ctx_spec,
                  pl.BlockSpec((None, NA_HEADS, LANES // GRID_W, GRID_W, NA_BAND_LANES),
                               lambda hg, b, rb: (l, hg, 0, 0, 0))],
        out_specs=q_spec,
        out_shape=jax.ShapeDtypeStruct((dec_batch * dec_seq, D_MODEL), BF16),
        scratch_shapes=[pltpu.VMEM((NA_HEADS * HEAD_DIM // LANES, dec_seq, HEAD_DIM), BF16)] * 2,
        compiler_params=_params("arbitrary", "arbitrary", "arbitrary"),
        name="attn_lat",
    )(q, k, v, kct, vct, bias)


def _na_bias_tables(rpb):
    c = np.arange(GRID_W)
    cs = np.clip(c - WIN_W // 2, 0, GRID_W - WIN_W)
    kc = np.arange(GRID_W)
    col_ok = (kc[None, :] >= cs[:, None]) & (kc[None, :] < cs[:, None] + WIN_W)
    dcol = kc[None, :] - c[:, None] + WIN_W - 1
    n_row, n_col = 2 * WIN_H - 1, 2 * WIN_W - 1
    band_rows = NA_BAND_LANES // GRID_W
    depth, heads = rpb.shape[0], rpb.shape[1]
    rpb2 = jnp.stack([jnp.pad(rpb[:, :, b:], ((0, 0), (0, 0), (0, band_rows - n_row + b), (0, 0)))
                      for b in range(LANES // GRID_W)], axis=2) * LOG2E
    rpb2 = jnp.concatenate([rpb2, jnp.full(rpb2.shape[:-1] + (1,), MASK_VALUE, F32)], axis=-1)
    onehot = np.concatenate([(np.arange(n_col)[:, None, None] == dcol[None]) & col_ok[None], ~col_ok[None]],
                            axis=0).astype(np.float32)
    band = jnp.einsum('lhbid,dck->lhbcik', rpb2, onehot, precision=lax.Precision.HIGHEST)
    return band.reshape(depth, heads, LANES // GRID_W, GRID_W, NA_BAND_LANES)


def _merge_kernel(*refs, lat_tiles, split_x):
    x_refs, refs = refs[:1 + split_x], refs[1 + split_x:]
    (u_ref, h1_ref, yl_ref, yc_ref, s_ref, w3_ref, wgc_ref, wgn_ref, wgs_ref, wo_ref, mod_ref, g2_ref,
     x1_ref, u2_ref) = refs
    u = u_ref[...]
    branches = ((h1_ref[...], wgc_ref), (_pick(lat_tiles, yl_ref, yc_ref), wgn_ref), (s_ref[...], wgs_ref))
    y = None
    for c in range(D_MODEL // MXU_COLS):
        cs = slice(c * MXU_COLS, (c + 1) * MXU_COLS)
        mixed = None
        for b, (inp, wg_ref) in enumerate(branches):
            term = jax.nn.sigmoid(_dot(u, wg_ref[:, cs])) * _dot(inp, w3_ref[b, :, cs])
            mixed = term if mixed is None else mixed + term
        part = _dot(mixed.astype(BF16), wo_ref[cs, :])
        y = part if y is None else y + part
    m = mod_ref[0]
    x = _pick(lat_tiles, *x_refs) if split_x else x_refs[0][...]
    x1 = x + m[2:3] * y
    x1_ref[...] = x1
    u2_ref[...] = _norm_mod(x1, g2_ref[...], m[4:5], m[3:4]).astype(BF16)


def _merge(x_parts, u, h1, ya_lat, ya_ctx, s, w3, w_in, w_o, mods, g2, l, mod_row):
    n, d = u.shape
    tm = TM_MERGE
    lat_tiles = ya_lat.shape[0] // tm
    split_x = len(x_parts) == 2
    gate_off = (w_in.shape[2] - 3 * d) // d
    tile = pl.BlockSpec((tm, d), lambda i: (i, 0))
    split = _split_specs(tm, lat_tiles)
    gate_specs = [_resident((None, d, d), functools.partial(lambda b, i: (l, 0, gate_off + b), b)) for b in range(3)]
    return pl.pallas_call(
        functools.partial(_merge_kernel, lat_tiles=lat_tiles, split_x=split_x),
        grid=(n // tm,),
        in_specs=(split if split_x else [tile]) + [tile, tile] + split + [tile]
                 + [_resident((None, 3, d, d), lambda i: (l, 0, 0, 0))] + gate_specs
                 + [_resident((None, d, d), lambda i: (l, 0, 0)),
                    pl.BlockSpec((1, 6, d), lambda i: (mod_row(i, tm), 0, 0)),
                    pl.BlockSpec((1, d), lambda i: (0, 0))],
        out_specs=[tile, tile],
        out_shape=[jax.ShapeDtypeStruct((n, d), F32), jax.ShapeDtypeStruct((n, d), BF16)],
        compiler_params=_params("parallel"),
        name="merge",
    )(*x_parts, u, h1, ya_lat, ya_ctx, s, w3, w_in, w_in, w_in, w_o, mods, g2)


def _ffn_kernel(x_ref, u_ref, wg_ref, wu_ref, wd_ref, mod_ref, gn_ref, modn_ref, *rest, final):
    u = u_ref[...]
    f = None
    for c in range(wg_ref.shape[1] // MXU_COLS):
        cs = slice(c * MXU_COLS, (c + 1) * MXU_COLS)
        hidden = _silu(_dot(u, wg_ref[:, cs])) * _dot(u, wu_ref[:, cs])
        part = _dot(hidden.astype(BF16), wd_ref[cs, :])
        f = part if f is None else f + part
    x2 = x_ref[...] + mod_ref[0][5:6] * f
    if final:
        (y_ref,) = rest
        xn = x2 * lax.rsqrt(jnp.mean(x2 * x2, axis=-1, keepdims=True) + EPS)
        y_ref[...] = xn * gn_ref[...]
    else:
        x2_ref, un_ref = rest
        mn = modn_ref[0]
        x2_ref[...] = x2
        un_ref[...] = _norm_mod(x2, gn_ref[...], mn[1:2], mn[0:1]).astype(BF16)


def _ffn(x, u, wg, wu, wd, mods, g_next, mods_next, l, mod_row, final, row_off=0, rows=None):
    d = x.shape[1]
    n = x.shape[0] if rows is None else rows
    tm = TM_MERGE
    dff = wg.shape[2]
    assert dff % MXU_COLS == 0
    toff = row_off // tm
    tile = pl.BlockSpec((tm, d), lambda i: (i + toff, 0))
    out_tile = pl.BlockSpec((tm, d), lambda i: (i, 0))
    modspec = pl.BlockSpec((1, 6, d), lambda i: (mod_row(i + toff, tm), 0, 0))
    if final:
        out_shape = jax.ShapeDtypeStruct((n, d), F32)
        out_specs = out_tile
    else:
        out_shape = [jax.ShapeDtypeStruct((n, d), F32), jax.ShapeDtypeStruct((n, d), BF16)]
        out_specs = [out_tile, out_tile]
    return pl.pallas_call(
        functools.partial(_ffn_kernel, final=final),
        grid=(n // tm,),
        in_specs=[tile, tile,
                  _resident((None, d, dff), lambda i: (l, 0, 0)),
                  _resident((None, d, dff), lambda i: (l, 0, 0)),
                  _resident((None, dff, d), lambda i: (l, 0, 0)),
                  modspec,
                  pl.BlockSpec((1, d), lambda i: (0, 0)),
                  modspec],
        out_specs=out_specs,
        out_shape=out_shape,
        compiler_params=_params("parallel"),
        name="ffn_final" if final else "ffn",
    )(x, u, wg, wu, wd, mods, g_next, mods_next)


def kernel(x_prompt, x_sample, cache_k, cache_v, c, c_ctx, w_ada, b_ada, norm1_g, w_in, conv_dw_w, conv_dw_b, conv_ln_g, conv_ln_b, conv_pw_w, sc_dw_w, sc_out_w, na_rpb, na_out_w, w_o, norm2_g, ffn_w_gate, ffn_w_up, ffn_w_down, final_g):
    batch, seq, d = x_prompt.shape
    dec_batch, dec_seq, _ = x_sample.shape
    depth = w_ada.shape[0]
    n_lat = dec_batch * dec_seq
    n_ctx = batch * seq
    grid_rows = dec_seq // GRID_W
    assert d == D_MODEL and w_in.shape[2] == 11 * d
    assert seq == TM_CONV and dec_seq % TM_MM == 0 and n_ctx % TM_MM == 0
    assert grid_rows % NA_QROWS == 0 and grid_rows >= WIN_H and n_lat % TM_MERGE == 0

    def mod_row(i, tm):
        return jnp.minimum((i * tm) // dec_seq, dec_batch)

    n_cond = -(-(dec_batch + 1) // 8) * 8
    cond = jnp.zeros((n_cond, d), F32).at[:dec_batch].set(c).at[dec_batch].set(c_ctx)
    mods = _adaln(cond, w_ada, b_ada).reshape(depth, n_cond, 6, d)

    w_in_b = w_in.astype(BF16)
    w3_b = jnp.stack([conv_pw_w, na_out_w, sc_out_w], axis=1).astype(BF16)
    w_o_b = w_o.astype(BF16)
    wg_b, wu_b, wd_b = ffn_w_gate.astype(BF16), ffn_w_up.astype(BF16), ffn_w_down.astype(BF16)
    bias = _na_bias_tables(na_rpb)
    kct, vct = jnp.swapaxes(cache_k, 3, 4), jnp.swapaxes(cache_v, 3, 4)

    x_parts = (x_sample.reshape(n_lat, d), x_prompt.reshape(n_ctx, d))
    u = _prenorm(*x_parts, norm1_g[0][None], mods[0], mod_row)
    us = []
    for l in range(depth):
        us.append(u)
        h0, q, k, v, t, bg = _proj(u, w_in_b, l)
        h1, s = _conv_mixers(h0, t, bg, conv_dw_w[l], conv_dw_b[l][None], conv_ln_g[l][None],
                             conv_ln_b[l][None], sc_dw_w[l], n_lat, dec_seq)
        ya_lat = _attn_lat(q, k, v, kct, vct, bias, l, dec_batch, dec_seq)
        ya_ctx = _attn_ctx(q, k, v, n_lat, batch, seq)
        x, u2 = _merge(x_parts, u, h1, ya_lat, ya_ctx, s, w3_b, w_in_b, w_o_b, mods[l], norm2_g[l][None],
                       l, mod_row)
        if l + 1 < depth:
            x, u = _ffn(x, u2, wg_b, wu_b, wd_b, mods[l], norm1_g[l + 1][None], mods[l + 1], l, mod_row, False)
            x_parts = (x,)
        else:
            ffn_args = (x, u2, wg_b, wu_b, wd_b, mods[l], final_g[None], mods[l], l, mod_row, True)
            y_sample = _ffn(*ffn_args, row_off=0, rows=n_lat)
            y_prompt = _ffn(*ffn_args, row_off=n_lat, rows=n_ctx)
    new_kv = _proj_kv_ctx(us, w_in_b, n_lat, batch, seq)

    return (y_prompt.reshape(batch, seq, d), y_sample.reshape(dec_batch, dec_seq, d), new_kv[0], new_kv[1])
```

```python
import functools

import numpy as np
import jax
import jax.numpy as jnp
from jax import lax
from jax.experimental import pallas as pl
from jax.experimental.pallas import tpu as pltpu

F32 = jnp.float32
BF16 = jnp.bfloat16

D_MODEL = 1024
HEADS = 16
HEAD_DIM = 64
GRID_W = 64
WIN_H = 8
WIN_W = 16
CONV_K = 31
SC_K = 3
EPS = 1e-6
LANES = 128
SUBLANES = 8
MXU_COLS = 256
HALO = 16
MASK_VALUE = -1e30
LOG2E = 1.4426950408889634

TM_MM = 1024
TM_MERGE = 512
TM_CONV = 256
TN_PROJ = 1024
NA_QROWS = 8
NA_HEADS = 8
CTX_LANES = 512
NA_BAND_LANES = 1024
VMEM_LIMIT = 48 * 1024 * 1024


def _dot(a, b):
    return jnp.dot(a, b, preferred_element_type=F32)


def _dot_nt(a, b):
    return lax.dot_general(a, b, (((1,), (1,)), ((), ())), preferred_element_type=F32)


def _silu(x):
    return x * jax.nn.sigmoid(x)


def _norm_mod(x, g, scale, shift):
    xn = x * lax.rsqrt(jnp.mean(x * x, axis=-1, keepdims=True) + EPS)
    return (xn * g) * (1.0 + scale) + shift


def _params(*sem):
    return pltpu.CompilerParams(dimension_semantics=sem, vmem_limit_bytes=VMEM_LIMIT)


def _resident(block_shape, index_map):
    return pl.BlockSpec(block_shape, index_map, pipeline_mode=pl.Buffered(1))


def _adaln_kernel(cond_ref, w_ref, b_ref, o_ref):
    a = _silu(cond_ref[...]).astype(BF16)
    o_ref[...] = _dot(a, w_ref[...].astype(BF16)) + b_ref[...]


def _adaln(cond, w_ada, b_ada):
    depth, d, n = w_ada.shape
    rows = cond.shape[0]
    tn = 1536
    return pl.pallas_call(
        _adaln_kernel,
        grid=(depth, n // tn),
        in_specs=[pl.BlockSpec((rows, d), lambda l, j: (0, 0)),
                  pl.BlockSpec((None, d, tn), lambda l, j: (l, 0, j)),
                  pl.BlockSpec((None, 1, tn), lambda l, j: (l, 0, j))],
        out_specs=pl.BlockSpec((None, rows, tn), lambda l, j: (l, 0, j)),
        out_shape=jax.ShapeDtypeStruct((depth, rows, n), F32),
        compiler_params=_params("parallel", "parallel"),
        name="adaln",
    )(cond, w_ada, b_ada.reshape(depth, 1, n))


def _split_specs(tm, lat_tiles):
    lat = pl.BlockSpec((tm, D_MODEL), lambda i: (jnp.minimum(i, lat_tiles - 1), 0))
    ctx = pl.BlockSpec((tm, D_MODEL), lambda i: (jnp.maximum(i - lat_tiles, 0), 0))
    return [lat, ctx]


def _pick(lat_tiles, lat_ref, ctx_ref):
    return jnp.where(pl.program_id(0) < lat_tiles, lat_ref[...], ctx_ref[...])


def _prenorm_kernel(xl_ref, xc_ref, g_ref, mod_ref, u_ref, *, lat_tiles):
    m = mod_ref[0]
    u_ref[...] = _norm_mod(_pick(lat_tiles, xl_ref, xc_ref), g_ref[...], m[1:2], m[0:1]).astype(BF16)


def _prenorm(x_lat, x_ctx, g, mods, mod_row):
    d = x_lat.shape[1]
    n = x_lat.shape[0] + x_ctx.shape[0]
    lat_tiles = x_lat.shape[0] // TM_MM
    return pl.pallas_call(
        functools.partial(_prenorm_kernel, lat_tiles=lat_tiles),
        grid=(n // TM_MM,),
        in_specs=_split_specs(TM_MM, lat_tiles)
                 + [pl.BlockSpec((1, d), lambda i: (0, 0)),
                    pl.BlockSpec((1, 6, d), lambda i: (mod_row(i, TM_MM), 0, 0))],
        out_specs=pl.BlockSpec((TM_MM, d), lambda i: (i, 0)),
        out_shape=jax.ShapeDtypeStruct((n, d), BF16),
        compiler_params=_params("parallel"),
        name="prenorm",
    )(x_lat, x_ctx, g, mods)


def _proj_kernel(u_ref, w_ref, h0_ref, q_ref, k_ref, v_ref, t_ref, bg_ref):
    u = u_ref[...]

    def seg(n, cs):
        return _dot(u, w_ref[:, n * D_MODEL + cs.start:n * D_MODEL + cs.stop])

    for c in range(D_MODEL // MXU_COLS):
        cs = slice(c * MXU_COLS, (c + 1) * MXU_COLS)
        h0_ref[:, cs] = (seg(0, cs) * jax.nn.sigmoid(seg(1, cs))).astype(BF16)
        q_ref[:, cs] = (seg(2, cs) * (HEAD_DIM ** -0.5 * LOG2E)).astype(BF16)
        k_ref[:, cs] = seg(3, cs).astype(BF16)
        v_ref[:, cs] = seg(4, cs).astype(BF16)
        bg_ref[:, cs] = seg(5, cs).astype(BF16)
        t_ref[:, cs] = (seg(6, cs) * seg(7, cs)).astype(BF16)


def _proj(u, w_in, l):
    n, d = u.shape
    tm = TM_MERGE
    tile = pl.BlockSpec((tm, d), lambda i: (i, 0))
    return pl.pallas_call(
        _proj_kernel,
        grid=(n // tm,),
        in_specs=[tile, _resident((None, d, 8 * d), lambda i: (l, 0, 0))],
        out_specs=[tile] * 6,
        out_shape=[jax.ShapeDtypeStruct((n, d), BF16)] * 6,
        compiler_params=_params("parallel"),
        name="proj",
    )(u, w_in)


def _proj_kv_ctx_kernel(*refs):
    u_refs, (wk_ref, wv_ref, k_ref, v_ref) = refs[:-4], refs[-4:]
    seq = k_ref.shape[2]
    u = u_refs[0][...]
    for layer in range(1, len(u_refs)):
        u = jnp.where(pl.program_id(0) == layer, u_refs[layer][...], u)
    rk = _dot(u, wk_ref[...])
    rv = _dot(u, wv_ref[...])
    for b in range(k_ref.shape[0]):
        for h in range(TN_PROJ // HEAD_DIM):
            sl = slice(h * HEAD_DIM, (h + 1) * HEAD_DIM)
            k_ref[b, h] = rk[b * seq:(b + 1) * seq, sl]
            v_ref[b, h] = rv[b * seq:(b + 1) * seq, sl]


def _proj_kv_ctx(us, w_in, n_lat, batch, seq):
    depth = len(us)
    heads_per = TN_PROJ // HEAD_DIM
    koff, voff = 3 * D_MODEL // TN_PROJ, 4 * D_MODEL // TN_PROJ
    nb = TM_MERGE // seq
    roff = n_lat // (nb * seq)
    out_spec = pl.BlockSpec((nb, None, heads_per, seq, HEAD_DIM), lambda l, j, b: (b, l, j, 0, 0))
    out_sds = jax.ShapeDtypeStruct((batch, depth, HEADS, seq, HEAD_DIM), F32)
    u_specs = [pl.BlockSpec((nb * seq, D_MODEL),
                            functools.partial(lambda a, l, j, b: (roff + jnp.where(l == a, b, 0), 0), a))
               for a in range(depth)]
    return pl.pallas_call(
        _proj_kv_ctx_kernel,
        grid=(depth, D_MODEL // TN_PROJ, batch // nb),
        in_specs=u_specs
                 + [pl.BlockSpec((None, D_MODEL, TN_PROJ), lambda l, j, b: (l, 0, j + koff)),
                    pl.BlockSpec((None, D_MODEL, TN_PROJ), lambda l, j, b: (l, 0, j + voff))],
        out_specs=[out_spec, out_spec],
        out_shape=[out_sds, out_sds],
        compiler_params=_params("parallel", "parallel", "arbitrary"),
        name="proj_kv_ctx",
    )(*us, w_in, w_in)


def _fill_window(win_ref, prev_ref, cur_ref, next_ref, first, last, tm):
    prev = prev_ref[...].astype(F32)
    nxt = next_ref[...].astype(F32)
    win_ref[0:HALO, :] = jnp.where(first, 0.0, prev)
    win_ref[HALO:HALO + tm, :] = cur_ref[...].astype(F32)
    win_ref[HALO + tm:HALO + tm + HALO, :] = jnp.where(last, 0.0, nxt)


def _depthwise(win_ref, sh_ref, w_ref, out_ref, taps, tm):
    base = HALO - (taps - 1) // 2
    rows = 64
    sh_rows = sh_ref.shape[1]
    shifts = sorted({(base + k) % SUBLANES for k in range(taps)} - {0})

    def body(cb, carry):
        col = pl.multiple_of(cb * LANES, LANES)
        for s in shifts:
            sh_ref[s] = win_ref[pl.ds(s, sh_rows), pl.ds(col, LANES)]
        for r0 in range(0, tm, rows):
            acc = None
            for k in range(taps):
                off = base + r0 + k
                s, aligned = off % SUBLANES, off - off % SUBLANES
                if s == 0:
                    xk = win_ref[pl.ds(aligned, rows), pl.ds(col, LANES)]
                else:
                    xk = sh_ref[s, pl.ds(aligned, rows), :]
                term = xk * w_ref[pl.ds(k, 1), pl.ds(col, LANES)]
                acc = term if acc is None else acc + term
            out_ref[pl.ds(r0, rows), pl.ds(col, LANES)] = acc
        return carry

    lax.fori_loop(0, D_MODEL // LANES, body, 0)


def _conv_kernel(hp_ref, hc_ref, hn_ref, tp_ref, tc_ref, tn_ref, bg_ref,
                 wdw_ref, bdw_ref, lng_ref, lnb_ref, wsc_ref,
                 h1_ref, s_ref, win_ref, sh_ref, acc_ref, *, lat_tiles, lat_tiles_per_seq):
    tm = TM_CONV
    i = pl.program_id(0)
    is_lat = i < lat_tiles
    pos = i % lat_tiles_per_seq
    first = jnp.logical_or(jnp.logical_not(is_lat), pos == 0)
    last = jnp.logical_or(jnp.logical_not(is_lat), pos == lat_tiles_per_seq - 1)

    _fill_window(win_ref, hp_ref, hc_ref, hn_ref, first, last, tm)
    _depthwise(win_ref, sh_ref, wdw_ref, acc_ref, CONV_K, tm)
    y = acc_ref[...] + bdw_ref[...]
    mu = jnp.mean(y, axis=-1, keepdims=True)
    dlt = y - mu
    var = jnp.mean(dlt * dlt, axis=-1, keepdims=True)
    h = (dlt * lax.rsqrt(var + EPS)) * lng_ref[...] + lnb_ref[...]
    h1_ref[...] = _silu(h).astype(BF16)

    _fill_window(win_ref, tp_ref, tc_ref, tn_ref, first, last, tm)
    _depthwise(win_ref, sh_ref, wsc_ref, acc_ref, SC_K, tm)
    s_ref[...] = (bg_ref[...].astype(F32) * acc_ref[...]).astype(BF16)


def _conv_mixers(h0, t, bg, wdw, bdw, lng, lnb, wsc, lat_tokens, lat_seq):
    n, d = h0.shape
    tm = TM_CONV
    per = tm // HALO
    nblk = n // HALO
    cur = pl.BlockSpec((tm, d), lambda i: (i, 0))
    prev = pl.BlockSpec((HALO, d), lambda i: (jnp.maximum(i * per - 1, 0), 0))
    nxt = pl.BlockSpec((HALO, d), lambda i: (jnp.minimum((i + 1) * per, nblk - 1), 0))
    vec = pl.BlockSpec((1, d), lambda i: (0, 0))
    kern = functools.partial(_conv_kernel, lat_tiles=lat_tokens // tm, lat_tiles_per_seq=lat_seq // tm)
    return pl.pallas_call(
        kern,
        grid=(n // tm,),
        in_specs=[prev, cur, nxt, prev, cur, nxt, cur,
                  pl.BlockSpec((CONV_K, d), lambda i: (0, 0)), vec, vec, vec,
                  pl.BlockSpec((SC_K, d), lambda i: (0, 0))],
        out_specs=[cur, cur],
        out_shape=[jax.ShapeDtypeStruct((n, d), BF16)] * 2,
        scratch_shapes=[pltpu.VMEM((tm + 2 * HALO, d), F32),
                        pltpu.VMEM((SUBLANES, tm + 2 * HALO - SUBLANES, LANES), F32),
                        pltpu.VMEM((tm, d), F32)],
        compiler_params=_params("parallel"),
        name="conv_mixers",
    )(h0, h0, h0, t, t, t, bg, wdw, bdw, lng, lnb, wsc)


def _fold_lanes(op, *arrays):
    tiles = [a[:, i * LANES:(i + 1) * LANES] for a in arrays for i in range(a.shape[1] // LANES)]
    return functools.reduce(op, tiles)


def _attn_ctx_kernel(q_ref, k_ref, v_ref, o_ref):
    heads = [slice(h * HEAD_DIM, (h + 1) * HEAD_DIM) for h in range(CTX_LANES // HEAD_DIM)]
    scores = [_dot_nt(q_ref[:, sl], k_ref[:, sl]) for sl in heads]
    probs = []
    for s in scores:
        e = jnp.exp2(s - jnp.max(_fold_lanes(jnp.maximum, s), axis=-1, keepdims=True))
        probs.append((e.astype(BF16), jnp.sum(_fold_lanes(jnp.add, e), axis=-1, keepdims=True)))
    outs = [_dot(p, v_ref[:, sl]) / den for (p, den), sl in zip(probs, heads)]
    o_ref[...] = jnp.concatenate(outs, axis=-1).astype(BF16)


def _attn_ctx(q, k, v, lat_tokens, batch, seq):
    roff = lat_tokens // seq
    spec = pl.BlockSpec((seq, CTX_LANES), lambda b, hg: (b + roff, hg))
    return pl.pallas_call(
        _attn_ctx_kernel,
        grid=(batch, D_MODEL // CTX_LANES),
        in_specs=[spec, spec, spec],
        out_specs=pl.BlockSpec((seq, CTX_LANES), lambda b, hg: (b, hg)),
        out_shape=jax.ShapeDtypeStruct((batch * seq, D_MODEL), BF16),
        compiler_params=_params("parallel", "parallel"),
        name="attn_ctx",
    )(q, k, v)


def _attn_lat_kernel(q_ref, k_ref, v_ref, kct_ref, vct_ref, bias_ref, o_ref, ko_ref, vo_ref, *, grid_rows):
    rb = pl.program_id(2)
    nloc = WIN_H * GRID_W
    per_tile = LANES // HEAD_DIM

    @pl.when(rb == 0)
    def _():
        for p in range(NA_HEADS // per_tile):
            ko_ref[p] = k_ref[:, p * LANES + HEAD_DIM:(p + 1) * LANES]
            vo_ref[p] = v_ref[:, p * LANES + HEAD_DIM:(p + 1) * LANES]

    def window(ref, odd_ref, h, start):
        if h % per_tile == 0:
            return ref[pl.ds(start, nloc), h * HEAD_DIM:(h + 1) * HEAD_DIM]
        return odd_ref[h // per_tile, pl.ds(start, nloc), :]

    starts, band_sel, band_off = [], [], []
    for j in range(NA_QROWS):
        r = rb * NA_QROWS + j
        rs = jnp.clip(r - WIN_H // 2, 0, grid_rows - WIN_H)
        starts.append(pl.multiple_of(rs * GRID_W, GRID_W))
        lane0 = (WIN_H - 1 - (r - rs)) * GRID_W
        band_sel.append((lane0 // GRID_W) % (LANES // GRID_W))
        band_off.append(pl.multiple_of((lane0 // LANES) * LANES, LANES))

    scores = []
    for h in range(NA_HEADS):
        q = q_ref[:, h * HEAD_DIM:(h + 1) * HEAD_DIM]
        s_ctx = _dot(q, kct_ref[h].astype(BF16))
        s_rows = []
        for j in range(NA_QROWS):
            bias = bias_ref[h, band_sel[j], :, pl.ds(band_off[j], nloc)]
            s_rows.append(_dot_nt(q[j * GRID_W:(j + 1) * GRID_W], window(k_ref, ko_ref, h, starts[j])) + bias)
        scores.append((jnp.concatenate(s_rows, axis=0), s_ctx))
    probs = []
    for s_loc, s_ctx in scores:
        m = jnp.max(_fold_lanes(jnp.maximum, s_loc, s_ctx), axis=-1, keepdims=True)
        e_loc = jnp.exp2(s_loc - m)
        e_ctx = jnp.exp2(s_ctx - m)
        den = jnp.sum(_fold_lanes(jnp.add, e_loc, e_ctx), axis=-1, keepdims=True)
        probs.append((e_loc.astype(BF16), e_ctx.astype(BF16), den))
    outs = []
    for h in range(NA_HEADS):
        p_loc, p_ctx, den = probs[h]
        o_rows = [_dot(p_loc[j * GRID_W:(j + 1) * GRID_W], window(v_ref, vo_ref, h, starts[j]))
                  for j in range(NA_QROWS)]
        o = jnp.concatenate(o_rows, axis=0) + _dot_nt(p_ctx, vct_ref[h].astype(BF16))
        outs.append(o / den)
    o_ref[...] = jnp.concatenate(outs, axis=-1).astype(BF16)


def _attn_lat(q, k, v, kct, vct, bias, l, dec_batch, dec_seq):
    grid_rows = dec_seq // GRID_W
    nrb = grid_rows // NA_QROWS
    tq = NA_QROWS * GRID_W
    width = NA_HEADS * HEAD_DIM
    past = kct.shape[4]
    seq_spec = pl.BlockSpec((dec_seq, width), lambda hg, b, rb: (b, hg))
    ctx_spec = pl.BlockSpec((None, None, NA_HEADS, HEAD_DIM, past), lambda hg, b, rb: (b, l, hg, 0, 0))
    q_spec = pl.BlockSpec((tq, width), lambda hg, b, rb: (b * nrb + rb, hg))
    return pl.pallas_call(
        functools.partial(_attn_lat_kernel, grid_rows=grid_rows),
        grid=(D_MODEL // width, dec_batch, nrb),
        in_specs=[q_spec, seq_spec, seq_spec, ctx_spec, ctx_spec,
                  pl.BlockSpec((None, NA_HEADS, LANES // GRID_W, GRID_W, NA_BAND_LANES),
                               lambda hg, b, rb: (l, hg, 0, 0, 0))],
        out_specs=q_spec,
        out_shape=jax.ShapeDtypeStruct((dec_batch * dec_seq, D_MODEL), BF16),
        scratch_shapes=[pltpu.VMEM((NA_HEADS * HEAD_DIM // LANES, dec_seq, HEAD_DIM), BF16)] * 2,
        compiler_params=_params("arbitrary", "arbitrary", "arbitrary"),
        name="attn_lat",
    )(q, k, v, kct, vct, bias)


def _na_bias_tables(rpb):
    c = np.arange(GRID_W)
    cs = np.clip(c - WIN_W // 2, 0, GRID_W - WIN_W)
    kc = np.arange(GRID_W)
    col_ok = (kc[None, :] >= cs[:, None]) & (kc[None, :] < cs[:, None] + WIN_W)
    dcol = kc[None, :] - c[:, None] + WIN_W - 1
    n_row, n_col = 2 * WIN_H - 1, 2 * WIN_W - 1
    band_rows = NA_BAND_LANES // GRID_W
    depth, heads = rpb.shape[0], rpb.shape[1]
    rpb2 = jnp.stack([jnp.pad(rpb[:, :, b:], ((0, 0), (0, 0), (0, band_rows - n_row + b), (0, 0)))
                      for b in range(LANES // GRID_W)], axis=2) * LOG2E
    rpb2 = jnp.concatenate([rpb2, jnp.full(rpb2.shape[:-1] + (1,), MASK_VALUE, F32)], axis=-1)
    onehot = np.concatenate([(np.arange(n_col)[:, None, None] == dcol[None]) & col_ok[None], ~col_ok[None]],
                            axis=0).astype(np.float32)
    band = jnp.einsum('lhbid,dck->lhbcik', rpb2, onehot, precision=lax.Precision.HIGHEST)
    return band.reshape(depth, heads, LANES // GRID_W, GRID_W, NA_BAND_LANES)


def _merge_kernel(*refs, lat_tiles, split_x):
    x_refs, refs = refs[:1 + split_x], refs[1 + split_x:]
    (u_ref, h1_ref, yl_ref, yc_ref, s_ref, w3_ref, wgc_ref, wgn_ref, wgs_ref, wo_ref, mod_ref, g2_ref,
     x1_ref, u2_ref) = refs
    u = u_ref[...]
    branches = ((h1_ref[...], wgc_ref), (_pick(lat_tiles, yl_ref, yc_ref), wgn_ref), (s_ref[...], wgs_ref))
    y = None
    for c in range(D_MODEL // MXU_COLS):
        cs = slice(c * MXU_COLS, (c + 1) * MXU_COLS)
        mixed = None
        for b, (inp, wg_ref) in enumerate(branches):
            term = jax.nn.sigmoid(_dot(u, wg_ref[:, cs])) * _dot(inp, w3_ref[b, :, cs])
            mixed = term if mixed is None else mixed + term
        part = _dot(mixed.astype(BF16), wo_ref[cs, :])
        y = part if y is None else y + part
    m = mod_ref[0]
    x = _pick(lat_tiles, *x_refs) if split_x else x_refs[0][...]
    x1 = x + m[2:3] * y
    x1_ref[...] = x1
    u2_ref[...] = _norm_mod(x1, g2_ref[...], m[4:5], m[3:4]).astype(BF16)


def _merge(x_parts, u, h1, ya_lat, ya_ctx, s, w3, w_in, w_o, mods, g2, l, mod_row):
    n, d = u.shape
    tm = TM_MERGE
    lat_tiles = ya_lat.shape[0] // tm
    split_x = len(x_parts) == 2
    gate_off = (w_in.shape[2] - 3 * d) // d
    tile = pl.BlockSpec((tm, d), lambda i: (i, 0))
    split = _split_specs(tm, lat_tiles)
    gate_specs = [_resident((None, d, d), functools.partial(lambda b, i: (l, 0, gate_off + b), b)) for b in range(3)]
    return pl.pallas_call(
        functools.partial(_merge_kernel, lat_tiles=lat_tiles, split_x=split_x),
        grid=(n // tm,),
        in_specs=(split if split_x else [tile]) + [tile, tile] + split + [tile]
                 + [_resident((None, 3, d, d), lambda i: (l, 0, 0, 0))] + gate_specs
                 + [_resident((None, d, d), lambda i: (l, 0, 0)),
                    pl.BlockSpec((1, 6, d), lambda i: (mod_row(i, tm), 0, 0)),
                    pl.BlockSpec((1, d), lambda i: (0, 0))],
        out_specs=[tile, tile],
        out_shape=[jax.ShapeDtypeStruct((n, d), F32), jax.ShapeDtypeStruct((n, d), BF16)],
        compiler_params=_params("parallel"),
        name="merge",
    )(*x_parts, u, h1, ya_lat, ya_ctx, s, w3, w_in, w_in, w_in, w_o, mods, g2)


def _ffn_kernel(x_ref, u_ref, wg_ref, wu_ref, wd_ref, mod_ref, gn_ref, modn_ref, *rest, final):
    u = u_ref[...]
    f = None
    for c in range(wg_ref.shape[1] // MXU_COLS):
        cs = slice(c * MXU_COLS, (c + 1) * MXU_COLS)
        hidden = _silu(_dot(u, wg_ref[:, cs])) * _dot(u, wu_ref[:, cs])
        part = _dot(hidden.astype(BF16), wd_ref[cs, :])
        f = part if f is None else f + part
    x2 = x_ref[...] + mod_ref[0][5:6] * f
    if final:
        (y_ref,) = rest
        xn = x2 * lax.rsqrt(jnp.mean(x2 * x2, axis=-1, keepdims=True) + EPS)
        y_ref[...] = xn * gn_ref[...]
    else:
        x2_ref, un_ref = rest
        mn = modn_ref[0]
        x2_ref[...] = x2
        un_ref[...] = _norm_mod(x2, gn_ref[...], mn[1:2], mn[0:1]).astype(BF16)


def _ffn(x, u, wg, wu, wd, mods, g_next, mods_next, l, mod_row, final, row_off=0, rows=None):
    d = x.shape[1]
    n = x.shape[0] if rows is None else rows
    tm = TM_MERGE
    dff = wg.shape[2]
    assert dff % MXU_COLS == 0
    toff = row_off // tm
    tile = pl.BlockSpec((tm, d), lambda i: (i + toff, 0))
    out_tile = pl.BlockSpec((tm, d), lambda i: (i, 0))
    modspec = pl.BlockSpec((1, 6, d), lambda i: (mod_row(i + toff, tm), 0, 0))
    if final:
        out_shape = jax.ShapeDtypeStruct((n, d), F32)
        out_specs = out_tile
    else:
        out_shape = [jax.ShapeDtypeStruct((n, d), F32), jax.ShapeDtypeStruct((n, d), BF16)]
        out_specs = [out_tile, out_tile]
    return pl.pallas_call(
        functools.partial(_ffn_kernel, final=final),
        grid=(n // tm,),
        in_specs=[tile, tile,
                  _resident((None, d, dff), lambda i: (l, 0, 0)),
                  _resident((None, d, dff), lambda i: (l, 0, 0)),
                  _resident((None, dff, d), lambda i: (l, 0, 0)),
                  modspec,
                  pl.BlockSpec((1, d), lambda i: (0, 0)),
                  modspec],
        out_specs=out_specs,
        out_shape=out_shape,
        compiler_params=_params("parallel"),
        name="ffn_final" if final else "ffn",
    )(x, u, wg, wu, wd, mods, g_next, mods_next)


def kernel(x_prompt, x_sample, cache_k, cache_v, c, c_ctx, w_ada, b_ada, norm1_g, w_in, conv_dw_w, conv_dw_b, conv_ln_g, conv_ln_b, conv_pw_w, sc_dw_w, sc_out_w, na_rpb, na_out_w, w_o, norm2_g, ffn_w_gate, ffn_w_up, ffn_w_down, final_g):
    batch, seq, d = x_prompt.shape
    dec_batch, dec_seq, _ = x_sample.shape
    depth = w_ada.shape[0]
    n_lat = dec_batch * dec_seq
    n_ctx = batch * seq
    grid_rows = dec_seq // GRID_W
    assert d == D_MODEL and w_in.shape[2] == 11 * d
    assert seq == TM_CONV and dec_seq % TM_MM == 0 and n_ctx % TM_MM == 0
    assert grid_rows % NA_QROWS == 0 and grid_rows >= WIN_H and n_lat % TM_MERGE == 0

    def mod_row(i, tm):
        return jnp.minimum((i * tm) // dec_seq, dec_batch)

    n_cond = -(-(dec_batch + 1) // 8) * 8
    cond = jnp.zeros((n_cond, d), F32).at[:dec_batch].set(c).at[dec_batch].set(c_ctx)
    mods = _adaln(cond, w_ada, b_ada).reshape(depth, n_cond, 6, d)

    w_in_b = w_in.astype(BF16)
    w3_b = jnp.stack([conv_pw_w, na_out_w, sc_out_w], axis=1).astype(BF16)
    w_o_b = w_o.astype(BF16)
    wg_b, wu_b, wd_b = ffn_w_gate.astype(BF16), ffn_w_up.astype(BF16), ffn_w_down.astype(BF16)
    bias = _na_bias_tables(na_rpb)
    kct, vct = jnp.swapaxes(cache_k, 3, 4), jnp.swapaxes(cache_v, 3, 4)

    x_parts = (x_sample.reshape(n_lat, d), x_prompt.reshape(n_ctx, d))
    u = _prenorm(*x_parts, norm1_g[0][None], mods[0], mod_row)
    us = []
    for l in range(depth):
        us.append(u)
        h0, q, k, v, t, bg = _proj(u, w_in_b, l)
        h1, s = _conv_mixers(h0, t, bg, conv_dw_w[l], conv_dw_b[l][None], conv_ln_g[l][None],
                             conv_ln_b[l][None], sc_dw_w[l], n_lat, dec_seq)
        ya_lat = _attn_lat(q, k, v, kct, vct, bias, l, dec_batch, dec_seq)
        ya_ctx = _attn_ctx(q, k, v, n_lat, batch, seq)
        x, u2 = _merge(x_parts, u, h1, ya_lat, ya_ctx, s, w3_b, w_in_b, w_o_b, mods[l], norm2_g[l][None],
                       l, mod_row)
        if l + 1 < depth:
            x, u = _ffn(x, u2, wg_b, wu_b, wd_b, mods[l], norm1_g[l + 1][None], mods[l + 1], l, mod_row, False)
            x_parts = (x,)
        else:
            ffn_args = (x, u2, wg_b, wu_b, wd_b, mods[l], final_g[None], mods[l], l, mod_row, True)
            y_sample = _ffn(*ffn_args, row_off=0, rows=n_lat)
            y_prompt = _ffn(*ffn_args, row_off=n_lat, rows=n_ctx)
    new_kv = _proj_kv_ctx(us, w_in_b, n_lat, batch, seq)

    return (y_prompt.reshape(batch, seq, d), y_sample.reshape(dec_batch, dec_seq, d), new_kv[0], new_kv[1])
```
